```python
import math, functools
import jax, jax.numpy as jnp
from jax import lax
import numpy as np

D_MODEL = 2048
BATCH = 2
SEQ = 4096
DEPTH = 4
DEC_BATCH = 8
DEC_SEQ = 8
PAST_LEN = 16384
PAGE_SIZE = 128

N_HEADS = 8
N_KV_HEADS = 2
HEAD_DIM = 128
D_ATTN = N_HEADS * HEAD_DIM
KV_W = N_KV_HEADS * HEAD_DIM
N_IDX_HEADS = 8
IDX_DIM = 64
IDX_SCALE = (N_IDX_HEADS * IDX_DIM) ** -0.5
TOPK_MAX = 256
Q_BLOCK = 128
SSD_HEADS = 16
SSD_HEAD_DIM = 64
D_SSD = SSD_HEADS * SSD_HEAD_DIM
SSD_GROUPS = 2
SSD_STATE = 128
CONV_W = 4
CONV_DIM = D_SSD + 2 * SSD_GROUPS * SSD_STATE
SSD_CHUNK = 128
DT_MIN = 0.001
DT_MAX = 0.1
POOL_WINDOWS = (2, 4, 8, 16)
POOL_W = D_MODEL // 2
POOL_GW = POOL_W // len(POOL_WINDOWS)
POOL_STATE = max(POOL_WINDOWS) - 1
SGU_W = D_MODEL // 2
SGU_GROUPS = 4
SGU_GW = SGU_W // SGU_GROUPS
SGU_CHUNK = 128
D_FF = 4 * D_MODEL
N_AB = (DEPTH + 1) // 2
N_CD = DEPTH // 2
ALPHA = (2 * DEPTH) ** 0.25
BETA = (8 * DEPTH) ** -0.25
EPS = 1e-5
AB_SIZES = (D_ATTN, KV_W, KV_W, N_IDX_HEADS * IDX_DIM, IDX_DIM, N_IDX_HEADS, D_SSD, CONV_DIM, SSD_HEADS)
D_IN_AB = sum(AB_SIZES)
D_IN_CD = POOL_W + 2 * SGU_W

kernel_name = 'hybrid_dsa_ssd_pool_sgu_decode_step'


def layer_norm(x, g, b):
    xf = x.astype(jnp.float32)
    mu = jnp.mean(xf, -1, keepdims=True)
    var = jnp.mean(jnp.square(xf - mu), -1, keepdims=True)
    return ((xf - mu) * lax.rsqrt(var + EPS) * g + b).astype(x.dtype)


def rms_norm(x, g):
    xf = x.astype(jnp.float32)
    return xf * lax.rsqrt(jnp.mean(jnp.square(xf), -1, keepdims=True) + EPS) * g


def sqrelu_mlp(x, w1, w2):
    return jnp.square(jax.nn.relu(x @ w1)) @ w2


def dsa_attend(q, qi, wi, qpos, k_idx, gather_kv, topk):
    b, T = q.shape[:2]
    L = k_idx.shape[1]
    dots = jnp.einsum('bthd,bsd->bths', qi, k_idx).astype(jnp.float32)
    score = jnp.einsum('bth,bths->bts', wi.astype(jnp.float32), jax.nn.relu(dots)) * IDX_SCALE
    key_pos = jnp.arange(L, dtype=jnp.int32)
    score = jnp.where(key_pos[None, None, :] <= qpos[None, :, None], score, -jnp.inf)
    _, sel = lax.top_k(score, topk)
    valid = sel <= qpos[None, :, None]
    ks, vs = gather_kv(sel)
    qg = q.reshape(b, T, N_KV_HEADS, N_HEADS // N_KV_HEADS, HEAD_DIM)
    logits = jnp.einsum('btgrd,btkgd->btgrk', qg, ks).astype(jnp.float32) * HEAD_DIM ** -0.5
    logits = jnp.where(valid[:, :, None, None, :], logits, -jnp.inf)
    p = jax.nn.softmax(logits, axis=-1).astype(vs.dtype)
    o = jnp.einsum('btgrk,btkgd->btgrd', p, vs)
    return o.reshape(b, T, D_ATTN)


def prompt_attend(q, k, v, qi, ki, wi):
    b, n = q.shape[:2]
    topk = min(TOPK_MAX, n // 4)
    blk = math.gcd(n, Q_BLOCK)
    nb = n // blk
    bidx = jnp.arange(b)[:, None, None]

    def gather(sel):
        return k[bidx, sel], v[bidx, sel]

    def to_blocks(a):
        return a.reshape(b, nb, blk, *a.shape[2:]).swapaxes(0, 1)

    pos = jnp.arange(n, dtype=jnp.int32).reshape(nb, blk)

    def one_block(args):
        qb, qib, wib, pb = args
        return dsa_attend(qb, qib, wib, pb, ki, gather, topk)

    o = lax.map(one_block, (to_blocks(q), to_blocks(qi), to_blocks(wi), pos))
    return o.swapaxes(0, 1).reshape(b, n, D_ATTN)


def sample_attend(cache_k_l, cache_v_l, cache_ki_l, page_table, q, k, v, qi, ki, wi):
    b, n = q.shape[:2]
    past = page_table.shape[1] * PAGE_SIZE
    topk = min(TOPK_MAX, (past + n) // 4)
    ki_past = cache_ki_l[page_table].reshape(b, past, IDX_DIM)
    ki_all = jnp.concatenate([ki_past, ki.astype(ki_past.dtype)], axis=1)
    bidx = jnp.arange(b)[:, None, None]

    def gather(sel):
        in_past = (sel < past)[..., None, None]
        ps = jnp.minimum(sel, past - 1)
        phys = page_table[bidx, ps // PAGE_SIZE]
        off = ps % PAGE_SIZE
        ns = jnp.clip(sel - past, 0, n - 1)
        kk = jnp.where(in_past, cache_k_l[phys, off], k[bidx, ns])
        vv = jnp.where(in_past, cache_v_l[phys, off], v[bidx, ns])
        return kk, vv

    qpos = past + jnp.arange(n, dtype=jnp.int32)
    return dsa_attend(q, qi, wi, qpos, ki_all, gather, topk)


def causal_dwconv(x_ext, w, bias, n):
    out = bias
    for j in range(CONV_W):
        out = out + w[j] * x_ext[:, j:j + n]
    return out


def ssd_scan(x, dt, a, bm, cm, h0):
    b, n = x.shape[:2]
    R = SSD_HEADS // SSD_GROUPS
    q = math.gcd(n, SSD_CHUNK)
    nc = n // q
    f32 = jnp.float32
    xc = x.astype(f32).reshape(b, nc, q, SSD_GROUPS, R, SSD_HEAD_DIM)
    dtc = dt.reshape(b, nc, q, SSD_GROUPS, R)
    bc = bm.astype(f32).reshape(b, nc, q, SSD_GROUPS, SSD_STATE)
    cc = cm.astype(f32).reshape(b, nc, q, SSD_GROUPS, SSD_STATE)
    acum = jnp.cumsum(dtc * a.reshape(SSD_GROUPS, R), axis=2)
    causal = jnp.tril(jnp.ones((q, q), bool))
    seg = acum[:, :, :, None] - acum[:, :, None]
    decay_ts = jnp.exp(jnp.where(causal[:, :, None, None], seg, -jnp.inf))
    cb = jnp.einsum('bctgn,bcsgn->bctsg', cc, bc)
    y_diag = jnp.einsum('bctsgr,bcsgrp->bctgrp', cb[..., None] * decay_ts * dtc[:, :, None], xc)
    decay_end = jnp.exp(acum[:, :, -1:] - acum)
    chunk_states = jnp.einsum('bcsgn,bcsgr,bcsgrp->bcgrpn', bc, decay_end * dtc, xc)
    chunk_decay = jnp.exp(acum[:, :, -1])

    def step(h, inp):
        st, dec = inp
        return h * dec[..., None, None] + st, h

    h_init = h0.astype(f32).reshape(b, SSD_GROUPS, R, SSD_HEAD_DIM, SSD_STATE)
    h_last, h_in = lax.scan(step, h_init, (jnp.moveaxis(chunk_states, 1, 0), jnp.moveaxis(chunk_decay, 1, 0)))
    h_in = jnp.moveaxis(h_in, 0, 1)
    y_off = jnp.einsum('bctgn,bcgrpn->bctgrp', cc, h_in) * jnp.exp(acum)[..., None]
    y = (y_diag + y_off).reshape(b, n, SSD_HEADS, SSD_HEAD_DIM)
    return y, h_last.reshape(b, SSD_HEADS, SSD_HEAD_DIM, SSD_STATE)


def ssd_branch(z, xbc_ext, dt_raw, h0, conv_w, conv_b, dt_bias, a_log, d_skip, ssd_g):
    b, n, _ = z.shape
    xbc = jax.nn.silu(causal_dwconv(xbc_ext, conv_w, conv_b, n))
    xs, bm, cm = jnp.split(xbc, [D_SSD, D_SSD + SSD_GROUPS * SSD_STATE], axis=-1)
    dt = jax.nn.softplus(dt_raw.astype(jnp.float32) + dt_bias.astype(jnp.float32))
    a = -jnp.exp(a_log.astype(jnp.float32))
    xh = xs.reshape(b, n, SSD_HEADS, SSD_HEAD_DIM)
    y, h_last = ssd_scan(xh, dt, a,
                         bm.reshape(b, n, SSD_GROUPS, SSD_STATE),
                         cm.reshape(b, n, SSD_GROUPS, SSD_STATE), h0)
    y = y + d_skip.astype(jnp.float32)[:, None] * xh.astype(jnp.float32)
    y = y.reshape(b, n, D_SSD) * jax.nn.silu(z.astype(jnp.float32))
    y = rms_norm(y, ssd_g).astype(z.dtype)
    return y, h_last.astype(h0.dtype), xbc_ext[:, -(CONV_W - 1):]


def ab_mixer(x, conv_prefix, h0, attend, w_in, conv_w, conv_b, dt_bias, a_log, d_skip, ssd_g, w_out):
    b, n, _ = x.shape
    h = x @ w_in
    q, k, v, qi, ki, wi, z, xbc, dt_raw = jnp.split(h, np.cumsum(AB_SIZES)[:-1].tolist(), axis=-1)
    q = q.reshape(b, n, N_HEADS, HEAD_DIM)
    k = k.reshape(b, n, N_KV_HEADS, HEAD_DIM)
    v = v.reshape(b, n, N_KV_HEADS, HEAD_DIM)
    qi = qi.reshape(b, n, N_IDX_HEADS, IDX_DIM)
    attn = attend(q, k, v, qi, ki, wi)
    xbc_ext = jnp.concatenate([conv_prefix.astype(xbc.dtype), xbc], axis=1)
    y_ssd, h_last, conv_new = ssd_branch(z, xbc_ext, dt_raw, h0, conv_w, conv_b, dt_bias, a_log, d_skip, ssd_g)
    out = jnp.concatenate([attn, y_ssd], axis=-1) @ w_out
    return out, k, v, ki, h_last, conv_new


def pool_mix(p_ext, n, pool_w, pool_scale):
    b, le, _ = p_ext.shape
    pg = p_ext.reshape(b, le, len(POOL_WINDOWS), POOL_GW)
    cs = jnp.cumsum(pg.astype(jnp.float32), axis=1)
    cs = jnp.concatenate([jnp.zeros_like(cs[:, :1]), cs], axis=1)
    end = jnp.arange(le - n + 1, le + 1)
    means = []
    for j, w in enumerate(POOL_WINDOWS):
        start = jnp.maximum(end - w, 0)
        cnt = jnp.minimum(end, w).astype(jnp.float32)
        means.append((cs[:, end, j] - cs[:, start, j]) / cnt[None, :, None])
    mean = jnp.stack(means, axis=2)
    d = (mean - pg[:, le - n:].astype(jnp.float32)).astype(p_ext.dtype)
    out = jnp.einsum('bngc,gcd->bngd', d, pool_w).reshape(b, n, POOL_W)
    return out * pool_scale


def sgu_mix(u, v, w_s, b_s):
    b, n, _ = u.shape
    L = -(-n // SGU_CHUNK) * SGU_CHUNK
    vp = jnp.pad(v, ((0, 0), (0, L - n), (0, 0))).reshape(b, L // SGU_CHUNK, SGU_CHUNK, SGU_GROUPS, SGU_GW)
    wm = w_s * jnp.tril(jnp.ones((SGU_CHUNK, SGU_CHUNK), w_s.dtype))
    s = jnp.einsum('gts,bcsgd->bctgd', wm, vp) + b_s.T[None, None, :, :, None]
    s = s.reshape(b, L, SGU_W)[:, :n]
    return u * s


def cd_mixer(x, pool_prefix, w_in, pool_w, pool_scale, sgu_ln_g, sgu_ln_b, sgu_w, sgu_b, w_out):
    n = x.shape[1]
    h = x @ w_in
    p, uv = h[..., :POOL_W], h[..., POOL_W:]
    p_ext = p if pool_prefix is None else jnp.concatenate([pool_prefix.astype(p.dtype), p], axis=1)
    pooled = pool_mix(p_ext, n, pool_w, pool_scale)
    uv = jax.nn.gelu(uv, approximate=False)
    u, v = uv[..., :SGU_W], uv[..., SGU_W:]
    v = layer_norm(v, sgu_ln_g, sgu_ln_b)
    s = sgu_mix(u, v, sgu_w, sgu_b)
    out = jnp.concatenate([pooled, s], axis=-1) @ w_out
    return out, p_ext[:, -POOL_STATE:], v


def setup_inputs(seed: int = 0) -> dict:
    key = jax.random.key(seed)
    keys = jax.random.split(key, 48)
    counter = [0]

    def nxt():
        k = keys[counter[0]]
        counter[0] += 1
        return k

    def nrm(shape, scale=1.0):
        return scale * jax.random.normal(nxt(), shape, jnp.float32)

    n_pages = PAST_LEN // PAGE_SIZE
    n_used = DEC_BATCH * n_pages
    n_pool = (5 * n_used + 3) // 4
    page_table = jax.random.permutation(nxt(), n_pool)[:n_used].reshape(DEC_BATCH, n_pages).astype(jnp.int32)

    u = jax.random.uniform(nxt(), (N_AB, SSD_HEADS), jnp.float32)
    dt0 = jnp.exp(u * (math.log(DT_MAX) - math.log(DT_MIN)) + math.log(DT_MIN))
    dt_bias = dt0 + jnp.log(-jnp.expm1(-dt0))
    a_log = jnp.log(jax.random.uniform(nxt(), (N_AB, SSD_HEADS), jnp.float32, 1.0, 16.0))

    return {
        'x_prompt': nrm((BATCH, SEQ, D_MODEL)),
        'x_sample': nrm((DEC_BATCH, DEC_SEQ, D_MODEL)),
        'cache_k': nrm((N_AB, n_pool, PAGE_SIZE, N_KV_HEADS, HEAD_DIM)),
        'cache_v': nrm((N_AB, n_pool, PAGE_SIZE, N_KV_HEADS, HEAD_DIM)),
        'cache_idx_k': nrm((N_AB, n_pool, PAGE_SIZE, IDX_DIM)),
        'state_ssm': nrm((N_AB, DEC_BATCH, SSD_HEADS, SSD_HEAD_DIM, SSD_STATE), 0.3),
        'state_conv': nrm((N_AB, DEC_BATCH, CONV_W - 1, CONV_DIM)),
        'state_pool': nrm((N_CD, DEC_BATCH, POOL_STATE, POOL_W)),
        'page_table': page_table,
        'ln1_g': 1.0 + nrm((DEPTH, D_MODEL), 0.02),
        'ln1_b': nrm((DEPTH, D_MODEL), 0.02),
        'ln2_g': 1.0 + nrm((DEPTH, D_MODEL), 0.02),
        'ln2_b': nrm((DEPTH, D_MODEL), 0.02),
        'w_in_ab': nrm((N_AB, D_MODEL, D_IN_AB), D_MODEL ** -0.5),
        'conv_w': nrm((N_AB, CONV_W, CONV_DIM), CONV_W ** -0.5),
        'conv_b': nrm((N_AB, CONV_DIM), 0.02),
        'dt_bias': dt_bias,
        'a_log': a_log,
        'd_skip': 1.0 + nrm((N_AB, SSD_HEADS), 0.1),
        'ssd_norm_g': 1.0 + nrm((N_AB, D_SSD), 0.02),
        'w_out_ab': nrm((N_AB, D_ATTN + D_SSD, D_MODEL), BETA * (D_ATTN + D_SSD) ** -0.5),
        'w_in_cd': nrm((N_CD, D_MODEL, D_IN_CD), D_MODEL ** -0.5),
        'pool_w': nrm((N_CD, len(POOL_WINDOWS), POOL_GW, POOL_GW), POOL_GW ** -0.5),
        'pool_scale': 1.0 + nrm((N_CD, POOL_W), 0.02),
        'sgu_ln_g': 1.0 + nrm((N_CD, SGU_W), 0.02),
        'sgu_ln_b': nrm((N_CD, SGU_W), 0.02),
        'sgu_w': nrm((N_CD, SGU_GROUPS, SGU_CHUNK, SGU_CHUNK), SGU_CHUNK ** -0.5),
        'sgu_b': 1.0 + nrm((N_CD, SGU_GROUPS, SGU_CHUNK), 0.02),
        'w_out_cd': nrm((N_CD, POOL_W + SGU_W, D_MODEL), BETA * (POOL_W + SGU_W) ** -0.5),
        'w_ff1': nrm((DEPTH, D_MODEL, D_FF), D_MODEL ** -0.5),
        'w_ff2': nrm((DEPTH, D_FF, D_MODEL), BETA * D_FF ** -0.5),
    }


def reference(x_prompt, x_sample, cache_k, cache_v, cache_idx_k, state_ssm, state_conv, state_pool, page_table,
              ln1_g, ln1_b, ln2_g, ln2_b, w_in_ab, conv_w, conv_b, dt_bias, a_log, d_skip, ssd_norm_g, w_out_ab,
              w_in_cd, pool_w, pool_scale, sgu_ln_g, sgu_ln_b, sgu_w, sgu_b, w_out_cd, w_ff1, w_ff2):
    yp, ys = x_prompt, x_sample
    kp_l, vp_l, ip_l, hp_l, cp_l, pp_l = [], [], [], [], [], []
    ks_l, vs_l, is_l, hs_l, cs_l, ps_l, gs_l = [], [], [], [], [], [], []
    for i in range(DEPTH):
        j = i // 2
        if i % 2 == 0:
            ab_w = (w_in_ab[j], conv_w[j], conv_b[j], dt_bias[j], a_log[j], d_skip[j], ssd_norm_g[j], w_out_ab[j])
            bp = yp.shape[0]
            conv0 = jnp.zeros((bp, CONV_W - 1, CONV_DIM), yp.dtype)
            h00 = jnp.zeros((bp, SSD_HEADS, SSD_HEAD_DIM, SSD_STATE), state_ssm.dtype)
            mp, kp, vp, ip, hp, cp = ab_mixer(yp, conv0, h00, prompt_attend, *ab_w)
            att_s = functools.partial(sample_attend, cache_k[j], cache_v[j], cache_idx_k[j], page_table)
            ms, ks_, vs_, is_, hs, cs = ab_mixer(ys, state_conv[j], state_ssm[j], att_s, *ab_w)
            kp_l.append(kp); vp_l.append(vp); ip_l.append(ip); hp_l.append(hp); cp_l.append(cp)
            ks_l.append(ks_); vs_l.append(vs_); is_l.append(is_); hs_l.append(hs); cs_l.append(cs)
        else:
            cd_w = (w_in_cd[j], pool_w[j], pool_scale[j], sgu_ln_g[j], sgu_ln_b[j], sgu_w[j], sgu_b[j], w_out_cd[j])
            mp, pp, _ = cd_mixer(yp, None, *cd_w)
            ms, ps, gs = cd_mixer(ys, state_pool[j], *cd_w)
            pp_l.append(pp); ps_l.append(ps); gs_l.append(gs)
        yp = layer_norm(ALPHA * yp + mp, ln1_g[i], ln1_b[i])
        ys = layer_norm(ALPHA * ys + ms, ln1_g[i], ln1_b[i])
        yp = layer_norm(ALPHA * yp + sqrelu_mlp(yp, w_ff1[i], w_ff2[i]), ln2_g[i], ln2_b[i])
        ys = layer_norm(ALPHA * ys + sqrelu_mlp(ys, w_ff1[i], w_ff2[i]), ln2_g[i], ln2_b[i])
    k_prompt = jnp.stack(kp_l)
    v_prompt = jnp.stack(vp_l)
    idx_k_prompt = jnp.stack(ip_l)
    ssm_prompt = jnp.stack(hp_l)
    conv_prompt = jnp.stack(cp_l)
    pool_prompt = jnp.stack(pp_l)
    k_sample = jnp.stack(ks_l)
    v_sample = jnp.stack(vs_l)
    idx_k_sample = jnp.stack(is_l)
    ssm_sample = jnp.stack(hs_l)
    conv_sample = jnp.stack(cs_l)
    pool_sample = jnp.stack(ps_l)
    sgu_v_sample = jnp.stack(gs_l)
    return (yp, ys, k_prompt, v_prompt, idx_k_prompt, ssm_prompt, conv_prompt, pool_prompt,
            k_sample, v_sample, idx_k_sample, ssm_sample, conv_sample, pool_sample, sgu_v_sample)
```

```python
import functools
import math

import jax
import jax.numpy as jnp
from jax import lax
from jax.experimental import pallas as pl
from jax.experimental.pallas import tpu as pltpu

D_MODEL = 2048
PAGE_SIZE = 128
N_HEADS = 8
N_KV_HEADS = 2
HEAD_DIM = 128
D_ATTN = N_HEADS * HEAD_DIM
KV_W = N_KV_HEADS * HEAD_DIM
N_IDX_HEADS = 8
IDX_DIM = 64
IDX_W = N_IDX_HEADS * IDX_DIM
IDX_SCALE = (N_IDX_HEADS * IDX_DIM) ** -0.5
TOPK_MAX = 256
Q_BLOCK = 128
SSD_HEADS = 16
SSD_HEAD_DIM = 64
D_SSD = SSD_HEADS * SSD_HEAD_DIM
SSD_GROUPS = 2
SSD_STATE = 128
CONV_W = 4
CONV_DIM = D_SSD + 2 * SSD_GROUPS * SSD_STATE
SSD_CHUNK = 128
POOL_WINDOWS = (2, 4, 8, 16)
POOL_W = D_MODEL // 2
POOL_GW = POOL_W // len(POOL_WINDOWS)
POOL_STATE = max(POOL_WINDOWS) - 1
SGU_W = D_MODEL // 2
SGU_GROUPS = 4
SGU_GW = SGU_W // SGU_GROUPS
SGU_CHUNK = 128
DEPTH = 4
ALPHA = (2 * DEPTH) ** 0.25
EPS = 1e-5
AB_SIZES = (D_ATTN, KV_W, KV_W, IDX_W, IDX_DIM, N_IDX_HEADS, D_SSD, CONV_DIM, SSD_HEADS)

AB_MAIN_W = D_ATTN + 2 * KV_W + IDX_W + D_SSD + CONV_DIM
AB_SMALL_W = 128
SMALL_KI = 0
SMALL_WI = IDX_DIM
SMALL_DT = IDX_DIM + N_IDX_HEADS

V7X_LANES = 128
V7X_SUBLANES = 8
V7X_VMEM_BYTES = 64 * 2**20

KEY_GROUP = 512
INT_MIN = -2**31
NEG_BIG = -1e30

F32 = jnp.float32
BF16 = jnp.bfloat16
I32 = jnp.int32


def _cparams(sem, vmem_bytes):
    return pltpu.CompilerParams(dimension_semantics=sem, vmem_limit_bytes=int(min(vmem_bytes, V7X_VMEM_BYTES - 4 * 2**20)))


def _dot(a, b):
    return jnp.dot(a, b, preferred_element_type=F32)


def _dot_nt(a, b):
    return lax.dot_general(a, b, (((1,), (1,)), ((), ())), preferred_element_type=F32)


def _dot_tn(a, b):
    return lax.dot_general(a, b, (((0,), (0,)), ((), ())), preferred_element_type=F32)


def _dot_exact(a, b):
    return jnp.dot(a, b, preferred_element_type=F32, precision=lax.Precision.HIGHEST)


def _layer_norm(x, g, b):
    mu = jnp.mean(x, axis=-1, keepdims=True)
    xc = x - mu
    var = jnp.mean(xc * xc, axis=-1, keepdims=True)
    return xc * lax.rsqrt(var + EPS) * g + b


def _inproj_ab_kernel(x_ref, w_ref, ws_ref, o_ref, os_ref, xb_ref):
    @pl.when(pl.program_id(1) == 0)
    def _():
        xb = x_ref[...].astype(BF16)
        xb_ref[...] = xb
        os_ref[...] = _dot(xb, ws_ref[...].astype(BF16))

    o_ref[...] = _dot(xb_ref[...], w_ref[...].astype(BF16))


def _inproj_ab(x, w_main, w_small, tm, tn):
    m, k = x.shape
    n = w_main.shape[1]
    ns = w_small.shape[1]
    vmem = 2 * tm * k * 4 + tm * k * 2 + 2 * k * tn * 4 + k * tn * 2 + 2 * tm * tn * 4 + 2 * k * ns * 4 + 2 * tm * ns * 4 + 4 * 2**20
    return pl.pallas_call(
        _inproj_ab_kernel,
        grid=(m // tm, n // tn),
        in_specs=[pl.BlockSpec((tm, k), lambda i, j: (i, 0)),
                  pl.BlockSpec((k, tn), lambda i, j: (0, j)),
                  pl.BlockSpec((k, ns), lambda i, j: (0, 0))],
        out_specs=[pl.BlockSpec((tm, tn), lambda i, j: (i, j)),
                   pl.BlockSpec((tm, ns), lambda i, j: (i, 0))],
        out_shape=[jax.ShapeDtypeStruct((m, n), F32), jax.ShapeDtypeStruct((m, ns), F32)],
        scratch_shapes=[pltpu.VMEM((tm, k), BF16)],
        compiler_params=_cparams(("parallel", "arbitrary"), vmem),
        name="inproj_ab",
    )(x, w_main, w_small)


def _inproj_kernel(x_ref, w_ref, o_ref, xb_ref):
    @pl.when(pl.program_id(1) == 0)
    def _():
        xb_ref[...] = x_ref[...].astype(BF16)

    o_ref[...] = _dot(xb_ref[...], w_ref[...].astype(BF16))


def _inproj(x, w, tm, tn):
    m, k = x.shape
    n = w.shape[1]
    vmem = 2 * tm * k * 4 + tm * k * 2 + 2 * k * tn * 4 + k * tn * 2 + 2 * tm * tn * 4 + 4 * 2**20
    return pl.pallas_call(
        _inproj_kernel,
        grid=(m // tm, n // tn),
        in_specs=[pl.BlockSpec((tm, k), lambda i, j: (i, 0)),
                  pl.BlockSpec((k, tn), lambda i, j: (0, j))],
        out_specs=pl.BlockSpec((tm, tn), lambda i, j: (i, j)),
        out_shape=jax.ShapeDtypeStruct((m, n), F32),
        scratch_shapes=[pltpu.VMEM((tm, k), BF16)],
        compiler_params=_cparams(("parallel", "arbitrary"), vmem),
        name="inproj_cd",
    )(x, w)


def _outproj_ln_kernel(a_ref, b_ref, w_ref, x_ref, g_ref, beta_ref, o_ref, wb_ref, *, slab):
    k = pl.program_id(1)
    tm = x_ref.shape[0]
    wb_ref[...] = w_ref[...].astype(BF16)

    @pl.when(k == 0)
    def _():
        def first(rows):
            o_ref[rows, :] = _dot(a_ref[rows, :], wb_ref[...])

        _row_slabs(tm, slab, first)

    @pl.when(k == 1)
    def _():
        def second(rows):
            y = ALPHA * x_ref[rows, :] + o_ref[rows, :] + _dot(b_ref[rows, :], wb_ref[...])
            o_ref[rows, :] = _layer_norm(y, g_ref[...], beta_ref[...])

        _row_slabs(tm, slab, second)


def _outproj_ln(a, b, w, x, g, beta, tm):
    m, d = x.shape
    kh = a.shape[1]
    slab = min(tm, 256)
    vmem = 4 * tm * kh * 2 + 2 * kh * d * 4 + kh * d * 2 + 4 * tm * d * 4 + 4 * slab * d * 4 + 4 * 2**20
    return pl.pallas_call(
        functools.partial(_outproj_ln_kernel, slab=slab),
        grid=(m // tm, 2),
        in_specs=[pl.BlockSpec((tm, kh), lambda i, k: (i, 0)),
                  pl.BlockSpec((tm, kh), lambda i, k: (i, 0)),
                  pl.BlockSpec((kh, d), lambda i, k: (k, 0)),
                  pl.BlockSpec((tm, d), lambda i, k: (i, 0)),
                  pl.BlockSpec((1, d), lambda i, k: (0, 0)),
                  pl.BlockSpec((1, d), lambda i, k: (0, 0))],
        out_specs=pl.BlockSpec((tm, d), lambda i, k: (i, 0)),
        out_shape=jax.ShapeDtypeStruct((m, d), F32),
        scratch_shapes=[pltpu.VMEM((kh, d), BF16)],
        compiler_params=_cparams(("parallel", "arbitrary"), vmem),
        name="outproj_ln",
    )(a, b, w, x, g, beta)


def _row_slabs(n_rows, slab, body):
    def step(r, carry):
        body(pl.ds(pl.multiple_of(r * slab, slab), slab))
        return carry

    lax.fori_loop(0, n_rows // slab, step, 0)


def _ffn_ln_kernel(x_ref, w1_ref, w2_ref, g_ref, beta_ref, o_ref, xb_ref, w1b_ref, w2b_ref, *, slab):
    f = pl.program_id(1)
    tm = x_ref.shape[0]

    @pl.when(f == 0)
    def _():
        def init(rows):
            xb_ref[rows, :] = x_ref[rows, :].astype(BF16)
            o_ref[rows, :] = jnp.zeros((slab, o_ref.shape[1]), F32)

        _row_slabs(tm, slab, init)

    w1b_ref[...] = w1_ref[...].astype(BF16)
    w2b_ref[...] = w2_ref[...].astype(BF16)

    def accumulate(rows):
        h = jnp.maximum(_dot(xb_ref[rows, :], w1b_ref[...]), 0.0)
        o_ref[rows, :] += _dot((h * h).astype(BF16), w2b_ref[...])

    _row_slabs(tm, slab, accumulate)

    @pl.when(f == pl.num_programs(1) - 1)
    def _():
        def finish(rows):
            o_ref[rows, :] = _layer_norm(ALPHA * x_ref[rows, :] + o_ref[rows, :], g_ref[...], beta_ref[...])

        _row_slabs(tm, slab, finish)


def _ffn_ln(x, w1, w2, g, beta, tm, tf):
    m, d = x.shape
    ff = w1.shape[1]
    slab = min(tm, 256)
    vmem = (2 * tm * d * 4 + tm * d * 2 + 4 * d * tf * 4 + 2 * d * tf * 2 + 2 * tm * d * 4
            + slab * tf * 8 + 4 * slab * d * 4 + 4 * 2**20)
    return pl.pallas_call(
        functools.partial(_ffn_ln_kernel, slab=slab),
        grid=(m // tm, ff // tf),
        in_specs=[pl.BlockSpec((tm, d), lambda i, f: (i, 0)),
                  pl.BlockSpec((d, tf), lambda i, f: (0, f)),
                  pl.BlockSpec((tf, d), lambda i, f: (f, 0)),
                  pl.BlockSpec((1, d), lambda i, f: (0, 0)),
                  pl.BlockSpec((1, d), lambda i, f: (0, 0))],
        out_specs=pl.BlockSpec((tm, d), lambda i, f: (i, 0)),
        out_shape=jax.ShapeDtypeStruct((m, d), F32),
        scratch_shapes=[pltpu.VMEM((tm, d), BF16), pltpu.VMEM((d, tf), BF16), pltpu.VMEM((tf, d), BF16)],
        compiler_params=_cparams(("parallel", "arbitrary"), vmem),
        name="ffn_ln",
    )(x, w1, w2, g, beta)


def _sortable_key(score):
    score = jnp.where(score == 0.0, 0.0, score)
    bits = lax.bitcast_convert_type(score, I32)
    return jnp.where(bits < 0, bits ^ jnp.int32(0x7FFFFFFF), bits)


def _fold_lanes(m):
    acc = m[:, 0:V7X_LANES]
    for c in range(1, m.shape[1] // V7X_LANES):
        acc = acc + m[:, c * V7X_LANES:(c + 1) * V7X_LANES]
    return acc


def _select_topk(load_group, n_groups, rows, topk, index_bits):
    def count(pred):
        def body(g, acc):
            return acc + _fold_lanes(pred(load_group(g), g).astype(I32))

        acc = lax.fori_loop(0, n_groups, body, jnp.zeros((rows, V7X_LANES), I32))
        return jnp.sum(acc, axis=1, keepdims=True)

    def thr_pass(p, thr):
        cand = thr + lax.shift_left(jnp.int32(1), jnp.int32(31) - p)
        c = count(lambda k, g: k >= cand)
        return jnp.where(c >= topk, cand, thr)

    thr = lax.fori_loop(0, 32, thr_pass, jnp.full((rows, 1), INT_MIN, I32))
    c_ge = count(lambda k, g: k >= thr)
    c_gt = count(lambda k, g: k > thr)
    need = topk - c_gt
    has_tie = jnp.max(jnp.where((c_ge > topk) & (thr > INT_MIN), 1, 0)) > 0
    all_idx = jnp.full((rows, 1), (1 << index_bits) - 1, I32)

    def tie_path():
        lane = lax.broadcasted_iota(I32, (rows, KEY_GROUP), 1)

        def idx_pass(p, last):
            cand = last + lax.shift_left(jnp.int32(1), jnp.int32(index_bits - 1) - p)
            c = count(lambda k, g: (k == thr) & (lane + g * KEY_GROUP < cand))
            return jnp.where(c <= need - 1, cand, last)

        return lax.fori_loop(0, index_bits, idx_pass, jnp.zeros((rows, 1), I32))

    last = lax.cond(has_tie, tie_path, lambda: all_idx)
    return thr, last


def _selected_bias(keys, idx, thr, last):
    sel = ((keys > thr) | ((keys == thr) & (idx <= last))) & (keys > INT_MIN)
    return jnp.where(sel, 0.0, NEG_BIG).astype(F32)


def _stack_heads(qb, group, rows_per_head):
    rep = N_HEADS // N_KV_HEADS
    return jnp.concatenate(
        [qb[:, (group * rep + r) * HEAD_DIM:(group * rep + r + 1) * HEAD_DIM] for r in range(rep)], axis=0)


def _softmax_step(s, v_bf16, m_ref, l_ref, acc_ref, g):
    m_old = m_ref[g]
    m_new = jnp.maximum(m_old, jnp.max(s, axis=1, keepdims=True))
    p = jnp.exp(s - m_new)
    alpha = jnp.exp(m_old - m_new)
    l_ref[g] = alpha * l_ref[g] + jnp.sum(p, axis=1, keepdims=True)
    acc_ref[g] = alpha * acc_ref[g] + _dot(p.astype(BF16), v_bf16)
    m_ref[g] = m_new


def _dsa_prompt_kernel(q_ref, qi_ref, sq_ref, kv_ref, sk_ref, o_ref,
                       kvb_ref, kib_ref, keys_ref, m_ref, l_ref, acc_ref, *, topk, index_bits):
    i = pl.program_id(1)
    tq = Q_BLOCK
    rep = N_HEADS // N_KV_HEADS

    @pl.when(i == 0)
    def _():
        kvb_ref[...] = kv_ref[...].astype(BF16)
        kib_ref[...] = sk_ref[:, SMALL_KI:SMALL_KI + IDX_DIM].astype(BF16)

    n_groups = lax.shift_right_logical(i * tq + tq + KEY_GROUP - 1, int(math.log2(KEY_GROUP)))
    q_pos = i * tq + lax.broadcasted_iota(I32, (tq, KEY_GROUP), 0)
    lane = lax.broadcasted_iota(I32, (tq, KEY_GROUP), 1)

    qib = qi_ref[...].astype(BF16)
    wi = sq_ref[:, SMALL_WI:SMALL_WI + N_IDX_HEADS]

    def score_body(g, carry):
        k0 = pl.multiple_of(g * KEY_GROUP, KEY_GROUP)
        kc = kib_ref[pl.ds(k0, KEY_GROUP), :]
        sc = jnp.zeros((tq, KEY_GROUP), F32)
        for h in range(N_IDX_HEADS):
            dots = _dot_nt(qib[:, h * IDX_DIM:(h + 1) * IDX_DIM], kc)
            sc = sc + wi[:, h:h + 1] * jnp.maximum(dots, 0.0)
        keys = _sortable_key(sc * IDX_SCALE)
        keys_ref[:, pl.ds(k0, KEY_GROUP)] = jnp.where(lane + k0 <= q_pos, keys, INT_MIN)
        return carry

    lax.fori_loop(0, n_groups, score_body, 0)

    def load_group(g):
        return keys_ref[:, pl.ds(pl.multiple_of(g * KEY_GROUP, KEY_GROUP), KEY_GROUP)]

    thr, last = _select_topk(load_group, n_groups, tq, topk, index_bits)

    qb = q_ref[...].astype(BF16)
    qg = [_stack_heads(qb, g, tq) for g in range(N_KV_HEADS)]
    m_ref[...] = jnp.full(m_ref.shape, NEG_BIG, F32)
    l_ref[...] = jnp.zeros(l_ref.shape, F32)
    acc_ref[...] = jnp.zeros(acc_ref.shape, F32)
    scale = HEAD_DIM ** -0.5

    def attn_body(g, carry):
        k0 = pl.multiple_of(g * KEY_GROUP, KEY_GROUP)
        bias = _selected_bias(keys_ref[:, pl.ds(k0, KEY_GROUP)], lane + k0, thr, last)
        bias = jnp.concatenate([bias] * rep, axis=0)
        for kvh in range(N_KV_HEADS):
            kc = kvb_ref[pl.ds(k0, KEY_GROUP), kvh * HEAD_DIM:(kvh + 1) * HEAD_DIM]
            vc = kvb_ref[pl.ds(k0, KEY_GROUP), KV_W + kvh * HEAD_DIM:KV_W + (kvh + 1) * HEAD_DIM]
            s = _dot_nt(qg[kvh], kc) * scale + bias
            _softmax_step(s, vc, m_ref, l_ref, acc_ref, kvh)
        return carry

    lax.fori_loop(0, n_groups, attn_body, 0)

    for kvh in range(N_KV_HEADS):
        out = acc_ref[kvh] / l_ref[kvh]
        for r in range(rep):
            h = kvh * rep + r
            o_ref[:, h * HEAD_DIM:(h + 1) * HEAD_DIM] = out[r * tq:(r + 1) * tq, :].astype(o_ref.dtype)


def _dsa_prompt(h_main, h_small, batch, seq):
    tq = Q_BLOCK
    nq = seq // tq
    topk = min(TOPK_MAX, seq // 4)
    rows = N_HEADS // N_KV_HEADS * tq
    kv_blk = 2 * KV_W
    kernel = functools.partial(_dsa_prompt_kernel, topk=topk, index_bits=int(math.log2(seq)))
    vmem = (2 * seq * kv_blk * 4 + seq * kv_blk * 2 + 2 * seq * AB_SMALL_W * 4 + seq * V7X_LANES * 2 + tq * seq * 4
            + 2 * tq * (D_ATTN + IDX_W + AB_SMALL_W) * 4 + 2 * tq * D_ATTN * 2
            + N_KV_HEADS * rows * (2 * V7X_LANES + HEAD_DIM) * 4 + 6 * rows * KEY_GROUP * 4 + 4 * 2**20)
    return pl.pallas_call(
        kernel,
        grid=(batch, nq),
        in_specs=[pl.BlockSpec((tq, D_ATTN), lambda b, i: (b * nq + i, 0)),
                  pl.BlockSpec((tq, IDX_W), lambda b, i: (b * nq + i, (D_ATTN + 2 * KV_W) // IDX_W)),
                  pl.BlockSpec((tq, AB_SMALL_W), lambda b, i: (b * nq + i, 0)),
                  pl.BlockSpec((seq, kv_blk), lambda b, i: (b, D_ATTN // kv_blk)),
                  pl.BlockSpec((seq, AB_SMALL_W), lambda b, i: (b, 0))],
        out_specs=pl.BlockSpec((tq, D_ATTN), lambda b, i: (b * nq + i, 0)),
        out_shape=jax.ShapeDtypeStruct((batch * seq, D_ATTN), BF16),
        scratch_shapes=[pltpu.VMEM((seq, kv_blk), BF16),
                        pltpu.VMEM((seq, IDX_DIM), BF16),
                        pltpu.VMEM((tq, seq), I32),
                        pltpu.VMEM((N_KV_HEADS, rows, 1), F32),
                        pltpu.VMEM((N_KV_HEADS, rows, 1), F32),
                        pltpu.VMEM((N_KV_HEADS, rows, HEAD_DIM), F32)],
        compiler_params=_cparams(("parallel", "arbitrary"), vmem),
        name="dsa_prompt",
    )(h_main, h_main, h_small, h_main, h_small)


def _dsa_sample_scores_kernel(pt_ref, qi_ref, sq_ref, *refs, pages_per_step):
    page_refs = refs[:pages_per_step]
    o_ref = refs[pages_per_step]
    n = qi_ref.shape[0]
    qib = qi_ref[...].astype(BF16)
    qs = jnp.concatenate([qib[:, h * IDX_DIM:(h + 1) * IDX_DIM] for h in range(N_IDX_HEADS)], axis=0)
    wi = sq_ref[:, SMALL_WI:SMALL_WI + N_IDX_HEADS]
    for t in range(pages_per_step):
        dots = jnp.maximum(_dot_nt(qs, page_refs[t][...].astype(BF16)), 0.0)
        sc = jnp.zeros((n, PAGE_SIZE), F32)
        for h in range(N_IDX_HEADS):
            sc = sc + wi[:, h:h + 1] * dots[h * n:(h + 1) * n, :]
        o_ref[:, t * PAGE_SIZE:(t + 1) * PAGE_SIZE] = sc * IDX_SCALE


def _dsa_sample_scores(page_table, h_main, h_small, cache_ki, n, pages_per_step):
    batch, n_pages = page_table.shape
    steps = n_pages // pages_per_step
    kernel = functools.partial(_dsa_sample_scores_kernel, pages_per_step=pages_per_step)

    def page_spec(t):
        return pl.BlockSpec((None, PAGE_SIZE, IDX_DIM), lambda b, s, pt: (pt[b, s * pages_per_step + t], 0, 0))

    grid_spec = pltpu.PrefetchScalarGridSpec(
        num_scalar_prefetch=1,
        grid=(batch, steps),
        in_specs=[pl.BlockSpec((n, IDX_W), lambda b, s, pt: (b, (D_ATTN + 2 * KV_W) // IDX_W)),
                  pl.BlockSpec((n, AB_SMALL_W), lambda b, s, pt: (b, 0))]
                 + [page_spec(t) for t in range(pages_per_step)],
        out_specs=pl.BlockSpec((None, n, pages_per_step * PAGE_SIZE), lambda b, s, pt: (b, 0, s)),
    )
    vmem = 4 * pages_per_step * PAGE_SIZE * V7X_LANES * 4 + 4 * n * pages_per_step * PAGE_SIZE * 4 + 8 * 2**20
    return pl.pallas_call(
        kernel,
        grid_spec=grid_spec,
        out_shape=jax.ShapeDtypeStruct((batch, n, n_pages * PAGE_SIZE), F32),
        compiler_params=_cparams(("parallel", "arbitrary"), vmem),
        name="dsa_sample_scores",
    )(page_table, h_main, h_small, *([cache_ki] * pages_per_step))


def _dsa_sample_attend_kernel(pt_ref, sc_ref, q_ref, qi_ref, sq_ref, kvn_ref, *refs,
                              pages_per_step, topk, index_bits, past):
    k_refs = refs[:pages_per_step]
    v_refs = refs[pages_per_step:2 * pages_per_step]
    o_ref = refs[2 * pages_per_step]
    keys_ref, thr_ref, last_ref, m_ref, l_ref, acc_ref, pad_ref = refs[2 * pages_per_step + 1:]
    s_idx = pl.program_id(1)
    n = q_ref.shape[0]
    rep = N_HEADS // N_KV_HEADS
    n_groups = keys_ref.shape[1] // KEY_GROUP
    lane_p = lax.broadcasted_iota(I32, (n, PAGE_SIZE), 1)
    scale = HEAD_DIM ** -0.5

    @pl.when(s_idx == 0)
    def _():
        keys_ref[...] = jnp.full(keys_ref.shape, INT_MIN, I32)
        keys_ref[:, 0:past] = _sortable_key(sc_ref[...])
        qib = qi_ref[...].astype(BF16)
        wi = sq_ref[:, SMALL_WI:SMALL_WI + N_IDX_HEADS]
        pad_ref[...] = jnp.zeros(pad_ref.shape, F32)
        pad_ref[0:n, 0:IDX_DIM] = sq_ref[:, SMALL_KI:SMALL_KI + IDX_DIM]
        ki_new = pad_ref[:, 0:IDX_DIM].astype(BF16)
        sc = jnp.zeros((n, PAGE_SIZE), F32)
        for h in range(N_IDX_HEADS):
            dots = _dot_nt(qib[:, h * IDX_DIM:(h + 1) * IDX_DIM], ki_new)
            sc = sc + wi[:, h:h + 1] * jnp.maximum(dots, 0.0)
        q_row = lax.broadcasted_iota(I32, (n, PAGE_SIZE), 0)
        keys_ref[:, past:past + PAGE_SIZE] = jnp.where(lane_p <= q_row, _sortable_key(sc * IDX_SCALE), INT_MIN)

        def load_group(g):
            return keys_ref[:, pl.ds(pl.multiple_of(g * KEY_GROUP, KEY_GROUP), KEY_GROUP)]

        thr, last = _select_topk(load_group, n_groups, n, topk, index_bits)
        thr_ref[...] = thr
        last_ref[...] = last
        m_ref[...] = jnp.full(m_ref.shape, NEG_BIG, F32)
        l_ref[...] = jnp.zeros(l_ref.shape, F32)
        acc_ref[...] = jnp.zeros(acc_ref.shape, F32)

    thr = thr_ref[...]
    last = last_ref[...]
    qb = q_ref[...].astype(BF16)
    qg = [_stack_heads(qb, g, n) for g in range(N_KV_HEADS)]

    def bias_at(k0):
        keys = keys_ref[:, pl.ds(k0, PAGE_SIZE)]
        b = _selected_bias(keys, lane_p + k0, thr, last)
        return jnp.concatenate([b] * rep, axis=0)

    base = s_idx * (pages_per_step * PAGE_SIZE)
    bias = jnp.concatenate([bias_at(pl.multiple_of(base + t * PAGE_SIZE, PAGE_SIZE)) for t in range(pages_per_step)], axis=1)
    for kvh in range(N_KV_HEADS):
        kc = jnp.concatenate([k_refs[t][:, kvh * HEAD_DIM:(kvh + 1) * HEAD_DIM].astype(BF16) for t in range(pages_per_step)], axis=0)
        vc = jnp.concatenate([v_refs[t][:, kvh * HEAD_DIM:(kvh + 1) * HEAD_DIM].astype(BF16) for t in range(pages_per_step)], axis=0)
        s = _dot_nt(qg[kvh], kc) * scale + bias
        _softmax_step(s, vc, m_ref, l_ref, acc_ref, kvh)

    @pl.when(s_idx == pl.num_programs(1) - 1)
    def _():
        bias_n = bias_at(past)
        for kvh in range(N_KV_HEADS):
            pad_ref[0:n, :] = kvn_ref[:, kvh * HEAD_DIM:(kvh + 1) * HEAD_DIM]
            kc = pad_ref[...].astype(BF16)
            pad_ref[0:n, :] = kvn_ref[:, KV_W + kvh * HEAD_DIM:KV_W + (kvh + 1) * HEAD_DIM]
            vc = pad_ref[...].astype(BF16)
            s = _dot_nt(qg[kvh], kc) * scale + bias_n
            _softmax_step(s, vc, m_ref, l_ref, acc_ref, kvh)
        for kvh in range(N_KV_HEADS):
            out = acc_ref[kvh] / l_ref[kvh]
            for r in range(rep):
                h = kvh * rep + r
                o_ref[:, h * HEAD_DIM:(h + 1) * HEAD_DIM] = out[r * n:(r + 1) * n, :].astype(o_ref.dtype)


def _dsa_sample_attend(page_table, scores, h_main, h_small, cache_k, cache_v, n, pages_per_step):
    batch, n_pages = page_table.shape
    past = n_pages * PAGE_SIZE
    steps = n_pages // pages_per_step
    topk = min(TOPK_MAX, (past + n) // 4)
    total = past + PAGE_SIZE
    padded = -(-total // KEY_GROUP) * KEY_GROUP
    index_bits = int(math.ceil(math.log2(padded)))
    rows = N_HEADS // N_KV_HEADS * n
    kernel = functools.partial(_dsa_sample_attend_kernel, pages_per_step=pages_per_step, topk=topk,
                               index_bits=index_bits, past=past)

    def page_spec(t):
        return pl.BlockSpec((None, PAGE_SIZE, KV_W), lambda b, s, pt: (pt[b, s * pages_per_step + t], 0, 0))

    kv_blk = 2 * KV_W
    grid_spec = pltpu.PrefetchScalarGridSpec(
        num_scalar_prefetch=1,
        grid=(batch, steps),
        in_specs=[pl.BlockSpec((None, n, past), lambda b, s, pt: (b, 0, 0)),
                  pl.BlockSpec((n, D_ATTN), lambda b, s, pt: (b, 0)),
                  pl.BlockSpec((n, IDX_W), lambda b, s, pt: (b, (D_ATTN + 2 * KV_W) // IDX_W)),
                  pl.BlockSpec((n, AB_SMALL_W), lambda b, s, pt: (b, 0)),
                  pl.BlockSpec((n, kv_blk), lambda b, s, pt: (b, D_ATTN // kv_blk))]
                 + [page_spec(t) for t in range(pages_per_step)] * 2,
        out_specs=pl.BlockSpec((n, D_ATTN), lambda b, s, pt: (b, 0)),
        scratch_shapes=[pltpu.VMEM((n, padded), I32),
                        pltpu.VMEM((n, 1), I32),
                        pltpu.VMEM((n, 1), I32),
                        pltpu.VMEM((N_KV_HEADS, rows, 1), F32),
                        pltpu.VMEM((N_KV_HEADS, rows, 1), F32),
                        pltpu.VMEM((N_KV_HEADS, rows, HEAD_DIM), F32),
                        pltpu.VMEM((PAGE_SIZE, HEAD_DIM), F32)],
    )
    vmem = (4 * 2 * pages_per_step * PAGE_SIZE * KV_W * 4 + 4 * n * padded * 4
            + 8 * rows * pages_per_step * PAGE_SIZE * 4 + 8 * 2**20)
    return pl.pallas_call(
        kernel,
        grid_spec=grid_spec,
        out_shape=jax.ShapeDtypeStruct((batch * n, D_ATTN), BF16),
        compiler_params=_cparams(("parallel", "arbitrary"), vmem),
        name="dsa_sample_attend",
    )(page_table, scores, h_main, h_main, h_small, h_main, *([cache_k] * pages_per_step), *([cache_v] * pages_per_step))


def _ssd_kernel(z_ref, xbc_ref, dtc_ref, dtr_ref, pre_ref, h0_ref, cw_ref, cb_ref, dtb_ref, alog_ref, dsk_ref,
                dtb_c_ref, alog_c_ref, g_ref, y_ref, hl_ref, state_ref, stage_ref, dtpad_ref, zpad_ref, *, t_in):
    c = pl.program_id(1)
    tt = SSD_CHUNK
    hpg = SSD_HEADS // SSD_GROUPS
    gw = hpg * SSD_HEAD_DIM

    @pl.when(c == 0)
    def _():
        state_ref[...] = h0_ref[...]
        stage_ref[0:V7X_SUBLANES, :] = pre_ref[...]
        if t_in < tt:
            stage_ref[V7X_SUBLANES:, :] = jnp.zeros((tt, CONV_DIM), F32)
            dtpad_ref[...] = jnp.zeros(dtpad_ref.shape, F32)
            zpad_ref[...] = jnp.zeros(zpad_ref.shape, F32)

    stage_ref[V7X_SUBLANES:V7X_SUBLANES + t_in, :] = xbc_ref[...]
    conv = cb_ref[...] + jnp.zeros((tt, CONV_DIM), F32)
    for j in range(CONV_W):
        off = V7X_SUBLANES - (CONV_W - 1) + j
        conv = conv + cw_ref[j:j + 1, :] * stage_ref[off:off + tt, :]
    if t_in == tt:
        stage_ref[0:V7X_SUBLANES, :] = xbc_ref[tt - V7X_SUBLANES:tt, :]
    xbc = conv * jax.nn.sigmoid(conv)
    xs = xbc[:, 0:D_SSD]

    row_valid = lax.broadcasted_iota(I32, (tt, 1), 0) < t_in
    lane_valid = lax.broadcasted_iota(I32, (1, tt), 1) < t_in
    if t_in < tt:
        dtpad_ref[0:t_in, :] = dtc_ref[...]
        dt_raw_c = dtpad_ref[...]
        dtpad_ref[0:SSD_HEADS, 0:t_in] = dtr_ref[...]
        dt_raw_r = dtpad_ref[0:SSD_HEADS, :]
        zpad_ref[0:t_in, :] = z_ref[...]
        z = zpad_ref[...]
    else:
        dt_raw_c = dtc_ref[...]
        dt_raw_r = dtr_ref[...]
        z = z_ref[...]

    hl = lax.broadcasted_iota(I32, (1, AB_SMALL_W), 1)
    head_lane = (hl >= SMALL_DT) & (hl < SMALL_DT + SSD_HEADS)
    dt_c = jnp.where(row_valid & head_lane, jax.nn.softplus(dt_raw_c + dtb_ref[...]), 0.0)
    dt_r = jnp.where(lane_valid, jax.nn.softplus(dt_raw_r + dtb_c_ref[...]), 0.0)
    a_c = -jnp.exp(alog_ref[...])
    a_r = -jnp.exp(alog_c_ref[...])
    ri = lax.broadcasted_iota(I32, (tt, tt), 0)
    ci = lax.broadcasted_iota(I32, (tt, tt), 1)
    tril = ri >= ci
    acum_c = _dot_exact(jnp.where(tril, 1.0, 0.0).astype(F32), dt_c * a_c)
    acum_r = _dot_exact(dt_r * a_r, jnp.where(ri <= ci, 1.0, 0.0).astype(F32))
    acum_last = acum_c[tt - 1:tt, :]

    eh = lax.broadcasted_iota(I32, (AB_SMALL_W, D_SSD), 0) - SMALL_DT
    ef = lax.broadcasted_iota(I32, (AB_SMALL_W, D_SSD), 1)
    expand = jnp.where(lax.shift_right_logical(ef, int(math.log2(SSD_HEAD_DIM))) == eh, 1.0, 0.0).astype(F32)
    decay_in = _dot_exact(jnp.exp(acum_c), expand)
    decay_out = _dot_exact(jnp.exp(acum_last - acum_c) * dt_c, expand)
    chunk_decay = _dot_exact(jnp.exp(acum_last) + jnp.zeros((V7X_SUBLANES, 1), F32), expand)[0:1, :]
    d_skip = _dot_exact(dsk_ref[...] + jnp.zeros((V7X_SUBLANES, 1), F32), expand)[0:1, :]

    xs_b = xs.astype(BF16)
    xw_b = (xs * decay_out).astype(BF16)
    lane128 = lax.broadcasted_iota(I32, (tt, 2 * SSD_HEAD_DIM), 1)
    y_parts = []
    new_state = []
    for g in range(SSD_GROUPS):
        bm = xbc[:, D_SSD + g * SSD_STATE:D_SSD + (g + 1) * SSD_STATE].astype(BF16)
        cm = xbc[:, D_SSD + (SSD_GROUPS + g) * SSD_STATE:D_SSD + (SSD_GROUPS + g + 1) * SSD_STATE].astype(BF16)
        cb = _dot_nt(cm, bm)
        st_g = state_ref[:, g * gw:(g + 1) * gw]
        y_off = _dot(cm, st_g.astype(BF16)) * decay_in[:, g * gw:(g + 1) * gw]
        new_state.append(st_g * chunk_decay[:, g * gw:(g + 1) * gw] + _dot_tn(bm, xw_b[:, g * gw:(g + 1) * gw]))
        for pair in range(hpg // 2):
            ms = []
            for hh in range(2):
                h = g * hpg + 2 * pair + hh
                seg = jnp.where(tril, acum_c[:, SMALL_DT + h:SMALL_DT + h + 1] - acum_r[h:h + 1, :], -jnp.inf)
                ms.append((cb * jnp.exp(seg) * dt_r[h:h + 1, :]).astype(BF16))
            col = (g * hpg + 2 * pair) * SSD_HEAD_DIM
            xp = xs_b[:, col:col + 2 * SSD_HEAD_DIM]
            zero = jnp.zeros_like(xp)
            rhs = jnp.concatenate([jnp.where(lane128 < SSD_HEAD_DIM, xp, zero),
                                   jnp.where(lane128 >= SSD_HEAD_DIM, xp, zero)], axis=0)
            y_parts.append(_dot(jnp.concatenate(ms, axis=1), rhs) + y_off[:, col - g * gw:col - g * gw + 2 * SSD_HEAD_DIM])
    for g in range(SSD_GROUPS):
        state_ref[:, g * gw:(g + 1) * gw] = new_state[g]

    y = jnp.concatenate(y_parts, axis=1) + d_skip * xs
    y = y * (z * jax.nn.sigmoid(z))
    y = y * lax.rsqrt(jnp.mean(y * y, axis=-1, keepdims=True) + EPS) * g_ref[...]
    y_ref[...] = y[0:t_in, :].astype(y_ref.dtype)

    @pl.when(c == pl.num_programs(1) - 1)
    def _():
        hl_ref[...] = state_ref[...]


def _ssd(h_main, h_small, dt_rows, prefix, h0, conv_w, conv_b, dt_bias, a_log, d_skip, ssd_g, batch, n_chunks, t_in):
    tt = SSD_CHUNK
    kernel = functools.partial(_ssd_kernel, t_in=t_in)
    z_blk = (D_ATTN + 2 * KV_W + IDX_W) // D_SSD
    xbc_blk = (D_ATTN + 2 * KV_W + IDX_W + D_SSD) // CONV_DIM
    row = lambda v: v.reshape(1, -1)
    col = lambda v: v.reshape(-1, 1)
    head_row = lambda v: jnp.pad(v.reshape(1, -1), ((0, 0), (SMALL_DT, AB_SMALL_W - SMALL_DT - SSD_HEADS)))
    full = lambda shape: pl.BlockSpec(shape, lambda b, c: (0,) * len(shape))
    vmem = 40 * tt * CONV_DIM * 4 + 8 * SSD_STATE * D_SSD * 4 + 8 * 2**20
    return pl.pallas_call(
        kernel,
        grid=(batch, n_chunks),
        in_specs=[pl.BlockSpec((t_in, D_SSD), lambda b, c: (b * n_chunks + c, z_blk)),
                  pl.BlockSpec((t_in, CONV_DIM), lambda b, c: (b * n_chunks + c, xbc_blk)),
                  pl.BlockSpec((t_in, AB_SMALL_W), lambda b, c: (b * n_chunks + c, 0)),
                  pl.BlockSpec((None, SSD_HEADS, t_in), lambda b, c: (b * n_chunks + c, 0, 0)),
                  pl.BlockSpec((None, V7X_SUBLANES, CONV_DIM), lambda b, c: (b, 0, 0)),
                  pl.BlockSpec((None, SSD_STATE, D_SSD), lambda b, c: (b, 0, 0)),
                  full((CONV_W, CONV_DIM)), full((1, CONV_DIM)), full((1, AB_SMALL_W)), full((1, AB_SMALL_W)),
                  full((1, AB_SMALL_W)), full((SSD_HEADS, 1)), full((SSD_HEADS, 1)), full((1, D_SSD))],
        out_specs=[pl.BlockSpec((t_in, D_SSD), lambda b, c: (b * n_chunks + c, 0)),
                   pl.BlockSpec((None, SSD_STATE, D_SSD), lambda b, c: (b, 0, 0))],
        out_shape=[jax.ShapeDtypeStruct((batch * n_chunks * t_in, D_SSD), BF16),
                   jax.ShapeDtypeStruct((batch, SSD_STATE, D_SSD), F32)],
        scratch_shapes=[pltpu.VMEM((SSD_STATE, D_SSD), F32),
                        pltpu.VMEM((V7X_SUBLANES + tt, CONV_DIM), F32),
                        pltpu.VMEM((tt, AB_SMALL_W), F32),
                        pltpu.VMEM((tt, D_SSD), F32)],
        compiler_params=_cparams(("parallel", "arbitrary"), vmem),
        name="ssd",
    )(h_main, h_main, h_small, dt_rows, prefix, h0, conv_w, row(conv_b), head_row(dt_bias), head_row(a_log), head_row(d_skip),
      col(dt_bias), col(a_log), row(ssd_g))


def _cd_kernel(p_ref, u_ref, v_ref, pre_ref, pw_ref, ps_ref, lg_ref, lb_ref, sw_ref, sb_ref,
               pooled_ref, gated_ref, vout_ref, stage_ref, upad_ref, vpad_ref, *, t_in, prefix_len):
    c = pl.program_id(1)
    tt = SGU_CHUNK
    hist = max(POOL_WINDOWS)

    @pl.when(c == 0)
    def _():
        stage_ref[0:hist, :] = pre_ref[...]
        if t_in < tt:
            stage_ref[hist:, :] = jnp.zeros((tt, POOL_W), F32)
            upad_ref[...] = jnp.zeros(upad_ref.shape, F32)
            vpad_ref[...] = jnp.zeros(vpad_ref.shape, F32)

    stage_ref[hist:hist + t_in, :] = p_ref[...]
    end = prefix_len + c * tt + lax.broadcasted_iota(I32, (tt, 1), 0) + 1
    pooled = []
    for j, w in enumerate(POOL_WINDOWS):
        cols = slice(j * POOL_GW, (j + 1) * POOL_GW)
        cur = stage_ref[hist:hist + tt, cols]
        tot = cur
        for k in range(1, w):
            tot = tot + stage_ref[hist - k:hist - k + tt, cols]
        cnt = jnp.minimum(end, w).astype(F32)
        d = (tot / cnt - cur).astype(BF16)
        pooled.append(_dot(d, pw_ref[j].astype(BF16)))
    if t_in == tt:
        stage_ref[0:hist, :] = p_ref[tt - hist:tt, :]
    pooled = jnp.concatenate(pooled, axis=1) * ps_ref[...]
    pooled_ref[...] = pooled[0:t_in, :].astype(pooled_ref.dtype)

    gelu = lambda x: x * (lax.erf(x / math.sqrt(2.0)) + 1.0) / 2.0
    u = gelu(u_ref[...])
    v = _layer_norm(gelu(v_ref[...]), lg_ref[...], lb_ref[...])
    vout_ref[...] = v
    if t_in < tt:
        upad_ref[0:t_in, :] = u
        vpad_ref[0:t_in, :] = v
        u = upad_ref[...]
        v = vpad_ref[...]
    ri = lax.broadcasted_iota(I32, (tt, tt), 0)
    ci = lax.broadcasted_iota(I32, (tt, tt), 1)
    vb = v.astype(BF16)
    gates = []
    for g in range(SGU_GROUPS):
        wm = jnp.where(ri >= ci, sw_ref[g], 0.0).astype(BF16)
        gates.append(_dot(wm, vb[:, g * SGU_GW:(g + 1) * SGU_GW]) + sb_ref[:, g:g + 1])
    gated = u * jnp.concatenate(gates, axis=1)
    gated_ref[...] = gated[0:t_in, :].astype(gated_ref.dtype)


def _cd_mixer(h, prefix, pool_w, pool_scale, ln_g, ln_b, sgu_w, sgu_b, batch, n_chunks, t_in, prefix_len):
    tt = SGU_CHUNK
    hist = max(POOL_WINDOWS)
    kernel = functools.partial(_cd_kernel, t_in=t_in, prefix_len=prefix_len)
    row = lambda v: v.reshape(1, -1)
    full = lambda shape: pl.BlockSpec(shape, lambda b, c: (0,) * len(shape))
    m = batch * n_chunks * t_in
    vmem = 40 * tt * POOL_W * 4 + 4 * POOL_GW * POOL_GW * len(POOL_WINDOWS) * 4 + 8 * 2**20
    return pl.pallas_call(
        kernel,
        grid=(batch, n_chunks),
        in_specs=[pl.BlockSpec((t_in, POOL_W), lambda b, c: (b * n_chunks + c, 0)),
                  pl.BlockSpec((t_in, SGU_W), lambda b, c: (b * n_chunks + c, 1)),
                  pl.BlockSpec((t_in, SGU_W), lambda b, c: (b * n_chunks + c, 2)),
                  pl.BlockSpec((None, hist, POOL_W), lambda b, c: (b, 0, 0)),
                  full((len(POOL_WINDOWS), POOL_GW, POOL_GW)), full((1, POOL_W)), full((1, SGU_W)), full((1, SGU_W)),
                  full((SGU_GROUPS, tt, tt)), full((tt, SGU_GROUPS))],
        out_specs=[pl.BlockSpec((t_in, POOL_W), lambda b, c: (b * n_chunks + c, 0)),
                   pl.BlockSpec((t_in, SGU_W), lambda b, c: (b * n_chunks + c, 0)),
                   pl.BlockSpec((t_in, SGU_W), lambda b, c: (b * n_chunks + c, 0))],
        out_shape=[jax.ShapeDtypeStruct((m, POOL_W), BF16),
                   jax.ShapeDtypeStruct((m, SGU_W), BF16),
                   jax.ShapeDtypeStruct((m, SGU_W), F32)],
        scratch_shapes=[pltpu.VMEM((hist + tt, POOL_W), F32),
                        pltpu.VMEM((tt, SGU_W), F32),
                        pltpu.VMEM((tt, SGU_W), F32)],
        compiler_params=_cparams(("parallel", "arbitrary"), vmem),
        name="cd_mixer",
    )(h, h, h, prefix, pool_w, row(pool_scale), row(ln_g), row(ln_b), sgu_w, sgu_b.T)


def _row_tile(m, cap):
    t = min(m, cap)
    while m % t:
        t //= 2
    return t


def _ab_layer(yp, ys, j, bp, sp, bs, ns, cache_k, cache_v, cache_idx_k, state_ssm, state_conv, page_table,
              w_in_ab, conv_w, conv_b, dt_bias, a_log, d_skip, ssd_norm_g, w_out_ab, ln_g, ln_b):
    o = [0]
    for s in AB_SIZES:
        o.append(o[-1] + s)
    w = w_in_ab[j]
    w_main = jnp.concatenate([w[:, o[0]:o[4]], w[:, o[6]:o[8]]], axis=1)
    w_small = jnp.concatenate([w[:, o[4]:o[6]], w[:, o[8]:o[9]],
                               jnp.zeros((D_MODEL, AB_SMALL_W - (o[6] - o[4]) - (o[9] - o[8])), w.dtype)], axis=1)
    outs = []
    states = []
    for x, batch, n in ((yp, bp, sp), (ys, bs, ns)):
        m = batch * n
        hm, hs = _inproj_ab(x, w_main, w_small, _row_tile(m, 1024), 512)
        prompt = n % SSD_CHUNK == 0
        t_in = SSD_CHUNK if prompt else n
        n_chunks = n // t_in
        dt_rows = hs[:, SMALL_DT:SMALL_DT + SSD_HEADS].reshape(batch * n_chunks, t_in, SSD_HEADS).transpose(0, 2, 1)
        if prompt:
            attn = _dsa_prompt(hm, hs, batch, n)
            prefix = jnp.zeros((batch, V7X_SUBLANES, CONV_DIM), F32)
            h0 = jnp.zeros((batch, SSD_STATE, D_SSD), F32)
        else:
            n_pool = cache_k.shape[1]
            pages = 16
            scores = _dsa_sample_scores(page_table, hm, hs, cache_idx_k[j], n, pages)
            attn = _dsa_sample_attend(page_table, scores, hm, hs, cache_k[j].reshape(n_pool, PAGE_SIZE, KV_W),
                                      cache_v[j].reshape(n_pool, PAGE_SIZE, KV_W), n, pages)
            prefix = jnp.concatenate([jnp.zeros((batch, V7X_SUBLANES - (CONV_W - 1), CONV_DIM), F32), state_conv[j]], axis=1)
            h0 = state_ssm[j].transpose(0, 3, 1, 2).reshape(batch, SSD_STATE, D_SSD)
        y_ssd, h_last = _ssd(hm, hs, dt_rows, prefix, h0, conv_w[j], conv_b[j], dt_bias[j], a_log[j], d_skip[j],
                             ssd_norm_g[j], batch, n_chunks, t_in)
        outs.append(_outproj_ln(attn, y_ssd, w_out_ab[j], x, ln_g.reshape(1, -1), ln_b.reshape(1, -1), _row_tile(m, 512)))
        k = hm[:, o[1]:o[2]].reshape(batch, n, N_KV_HEADS, HEAD_DIM)
        v = hm[:, o[2]:o[3]].reshape(batch, n, N_KV_HEADS, HEAD_DIM)
        ki = hs[:, SMALL_KI:SMALL_KI + IDX_DIM].reshape(batch, n, IDX_DIM)
        ssm = h_last.reshape(batch, SSD_STATE, SSD_HEADS, SSD_HEAD_DIM).transpose(0, 2, 3, 1)
        xbc_off = o[4] + D_SSD
        conv_new = hm[:, xbc_off:xbc_off + CONV_DIM].reshape(batch, n, CONV_DIM)[:, n - (CONV_W - 1):]
        states.append((k, v, ki, ssm, conv_new))
    return outs, states


def _cd_layer(yp, ys, j, bp, sp, bs, ns, state_pool, w_in_cd, pool_w, pool_scale, sgu_ln_g, sgu_ln_b, sgu_w, sgu_b,
              w_out_cd, ln_g, ln_b):
    hist = max(POOL_WINDOWS)
    outs = []
    states = []
    for x, batch, n in ((yp, bp, sp), (ys, bs, ns)):
        m = batch * n
        h = _inproj(x, w_in_cd[j], _row_tile(m, 1024), 512)
        prompt = n % SGU_CHUNK == 0
        t_in = SGU_CHUNK if prompt else n
        n_chunks = n // t_in
        p = h[:, 0:POOL_W].reshape(batch, n, POOL_W)
        if prompt:
            prefix = jnp.zeros((batch, hist, POOL_W), F32)
            prefix_len = 0
            pool_state = p[:, n - POOL_STATE:]
        else:
            prefix = jnp.concatenate([jnp.zeros((batch, hist - POOL_STATE, POOL_W), F32), state_pool[j]], axis=1)
            prefix_len = POOL_STATE
            pool_state = jnp.concatenate([state_pool[j], p], axis=1)[:, n:]
        pooled, gated, v = _cd_mixer(h, prefix, pool_w[j], pool_scale[j], sgu_ln_g[j], sgu_ln_b[j], sgu_w[j], sgu_b[j],
                                     batch, n_chunks, t_in, prefix_len)
        outs.append(_outproj_ln(pooled, gated, w_out_cd[j], x, ln_g.reshape(1, -1), ln_b.reshape(1, -1), _row_tile(m, 512)))
        states.append((pool_state, v.reshape(batch, n, SGU_W)))
    return outs, states


def kernel(x_prompt, x_sample, cache_k, cache_v, cache_idx_k, state_ssm, state_conv, state_pool, page_table, ln1_g, ln1_b, ln2_g, ln2_b, w_in_ab, conv_w, conv_b, dt_bias, a_log, d_skip, ssd_norm_g, w_out_ab, w_in_cd, pool_w, pool_scale, sgu_ln_g, sgu_ln_b, sgu_w, sgu_b, w_out_cd, w_ff1, w_ff2):
    bp, sp, d = x_prompt.shape
    bs, ns, _ = x_sample.shape
    depth = w_ff1.shape[0]
    yp = x_prompt.reshape(bp * sp, d)
    ys = x_sample.reshape(bs * ns, d)
    ab_p, ab_s, cd_p, cd_s = [], [], [], []
    for i in range(depth):
        j = i // 2
        if i % 2 == 0:
            (yp, ys), (st_p, st_s) = _ab_layer(yp, ys, j, bp, sp, bs, ns, cache_k, cache_v, cache_idx_k, state_ssm,
                                               state_conv, page_table, w_in_ab, conv_w, conv_b, dt_bias, a_log, d_skip,
                                               ssd_norm_g, w_out_ab, ln1_g[i], ln1_b[i])
            ab_p.append(st_p)
            ab_s.append(st_s)
        else:
            (yp, ys), (st_p, st_s) = _cd_layer(yp, ys, j, bp, sp, bs, ns, state_pool, w_in_cd, pool_w, pool_scale,
                                               sgu_ln_g, sgu_ln_b, sgu_w, sgu_b, w_out_cd, ln1_g[i], ln1_b[i])
            cd_p.append(st_p)
            cd_s.append(st_s)
        g2, b2 = ln2_g[i].reshape(1, -1), ln2_b[i].reshape(1, -1)
        yp = _ffn_ln(yp, w_ff1[i], w_ff2[i], g2, b2, _row_tile(bp * sp, 1024), 256)
        ys = _ffn_ln(ys, w_ff1[i], w_ff2[i], g2, b2, _row_tile(bs * ns, 1024), 256)
    stack = lambda lst, idx: jnp.stack([t[idx] for t in lst])
    return (yp.reshape(bp, sp, d), ys.reshape(bs, ns, d),
            stack(ab_p, 0), stack(ab_p, 1), stack(ab_p, 2), stack(ab_p, 3), stack(ab_p, 4), stack(cd_p, 0),
            stack(ab_s, 0), stack(ab_s, 1), stack(ab_s, 2), stack(ab_s, 3), stack(ab_s, 4), stack(cd_s, 0), stack(cd_s, 1))
```

```python
import functools
import math

import jax
import jax.numpy as jnp
from jax import lax
from jax.experimental import pallas as pl
from jax.experimental.pallas import tpu as pltpu

D_MODEL = 2048
PAGE_SIZE = 128
N_HEADS = 8
N_KV_HEADS = 2
HEAD_DIM = 128
D_ATTN = N_HEADS * HEAD_DIM
KV_W = N_KV_HEADS * HEAD_DIM
N_IDX_HEADS = 8
IDX_DIM = 64
IDX_W = N_IDX_HEADS * IDX_DIM
IDX_SCALE = (N_IDX_HEADS * IDX_DIM) ** -0.5
TOPK_MAX = 256
Q_BLOCK = 128
SSD_HEADS = 16
SSD_HEAD_DIM = 64
D_SSD = SSD_HEADS * SSD_HEAD_DIM
SSD_GROUPS = 2
SSD_STATE = 128
CONV_W = 4
CONV_DIM = D_SSD + 2 * SSD_GROUPS * SSD_STATE
SSD_CHUNK = 128
POOL_WINDOWS = (2, 4, 8, 16)
POOL_W = D_MODEL // 2
POOL_GW = POOL_W // len(POOL_WINDOWS)
POOL_STATE = max(POOL_WINDOWS) - 1
SGU_W = D_MODEL // 2
SGU_GROUPS = 4
SGU_GW = SGU_W // SGU_GROUPS
SGU_CHUNK = 128
DEPTH = 4
ALPHA = (2 * DEPTH) ** 0.25
EPS = 1e-5
AB_SIZES = (D_ATTN, KV_W, KV_W, IDX_W, IDX_DIM, N_IDX_HEADS, D_SSD, CONV_DIM, SSD_HEADS)

AB_MAIN_W = D_ATTN + 2 * KV_W + IDX_W + D_SSD + CONV_DIM
AB_SMALL_W = 128
SMALL_KI = 0
SMALL_WI = IDX_DIM
SMALL_DT = IDX_DIM + N_IDX_HEADS

V7X_LANES = 128
V7X_SUBLANES = 8
V7X_VMEM_BYTES = 64 * 2**20

KEY_GROUP = 512
INT_MIN = -2**31
NEG_BIG = -1e30

F32 = jnp.float32
BF16 = jnp.bfloat16
I32 = jnp.int32


def _cparams(sem, vmem_bytes):
    return pltpu.CompilerParams(dimension_semantics=sem, vmem_limit_bytes=int(min(vmem_bytes, V7X_VMEM_BYTES - 4 * 2**20)))


def _dot(a, b):
    return jnp.dot(a, b, preferred_element_type=F32)


def _dot_nt(a, b):
    return lax.dot_general(a, b, (((1,), (1,)), ((), ())), preferred_element_type=F32)


def _dot_tn(a, b):
    return lax.dot_general(a, b, (((0,), (0,)), ((), ())), preferred_element_type=F32)


def _dot_exact(a, b):
    return jnp.dot(a, b, preferred_element_type=F32, precision=lax.Precision.HIGHEST)


def _layer_norm(x, g, b):
    mu = jnp.mean(x, axis=-1, keepdims=True)
    xc = x - mu
    var = jnp.mean(xc * xc, axis=-1, keepdims=True)
    return xc * lax.rsqrt(var + EPS) * g + b


def _inproj_ab_kernel(x_ref, wh_ref, wr_ref, ws_ref, o_ref, os_ref, xb_ref, *, n_head):
    j = pl.program_id(1)

    @pl.when(j == 0)
    def _():
        xb = x_ref[...].astype(BF16)
        xb_ref[...] = xb
        os_ref[...] = _dot(xb, ws_ref[...].astype(BF16))

    @pl.when(j < n_head)
    def _():
        o_ref[...] = _dot(xb_ref[...], wh_ref[...].astype(BF16))

    @pl.when(j >= n_head)
    def _():
        o_ref[...] = _dot(xb_ref[...], wr_ref[...].astype(BF16))


def _inproj_ab(x, w_all, w_rest, layer, tm, tn):
    m, k = x.shape
    head_w = D_ATTN + 2 * KV_W + IDX_W
    n_head = head_w // tn
    rest_w = AB_MAIN_W - head_w
    vmem = 2 * tm * k * 4 + tm * k * 2 + 4 * k * tn * 4 + k * tn * 2 + 2 * tm * tn * 4 + 2 * k * AB_SMALL_W * 4 + 2 * tm * AB_SMALL_W * 4 + 4 * 2**20
    return pl.pallas_call(
        functools.partial(_inproj_ab_kernel, n_head=n_head),
        grid=(m // tm, AB_MAIN_W // tn),
        in_specs=[pl.BlockSpec((tm, k), lambda i, j: (i, 0)),
                  pl.BlockSpec((None, k, tn), lambda i, j: (layer, 0, jnp.minimum(j, n_head - 1))),
                  pl.BlockSpec((k, tn), lambda i, j: (0, jnp.maximum(j - n_head, 0))),
                  pl.BlockSpec((k, AB_SMALL_W), lambda i, j: (0, rest_w // AB_SMALL_W))],
        out_specs=[pl.BlockSpec((tm, tn), lambda i, j: (i, j)),
                   pl.BlockSpec((tm, AB_SMALL_W), lambda i, j: (i, 0))],
        out_shape=[jax.ShapeDtypeStruct((m, AB_MAIN_W), F32), jax.ShapeDtypeStruct((m, AB_SMALL_W), F32)],
        scratch_shapes=[pltpu.VMEM((tm, k), BF16)],
        compiler_params=_cparams(("parallel", "arbitrary"), vmem),
        name="inproj_ab",
    )(x, w_all, w_rest, w_rest)


def _inproj_kernel(x_ref, w_ref, o_ref, xb_ref):
    @pl.when(pl.program_id(1) == 0)
    def _():
        xb_ref[...] = x_ref[...].astype(BF16)

    o_ref[...] = _dot(xb_ref[...], w_ref[...].astype(BF16))


def _inproj(x, w_all, layer, tm, tn):
    m, k = x.shape
    n = w_all.shape[2]
    vmem = 2 * tm * k * 4 + tm * k * 2 + 2 * k * tn * 4 + k * tn * 2 + 2 * tm * tn * 4 + 4 * 2**20
    return pl.pallas_call(
        _inproj_kernel,
        grid=(m // tm, n // tn),
        in_specs=[pl.BlockSpec((tm, k), lambda i, j: (i, 0)),
                  pl.BlockSpec((None, k, tn), lambda i, j: (layer, 0, j))],
        out_specs=pl.BlockSpec((tm, tn), lambda i, j: (i, j)),
        out_shape=jax.ShapeDtypeStruct((m, n), F32),
        scratch_shapes=[pltpu.VMEM((tm, k), BF16)],
        compiler_params=_cparams(("parallel", "arbitrary"), vmem),
        name="inproj_cd",
    )(x, w_all)


def _outproj_ln_kernel(a_ref, b_ref, w_ref, x_ref, g_ref, beta_ref, o_ref, wb_ref, *, slab):
    k = pl.program_id(1)
    tm = x_ref.shape[0]
    wb_ref[...] = w_ref[...].astype(BF16)

    @pl.when(k == 0)
    def _():
        def first(rows):
            o_ref[rows, :] = _dot(a_ref[rows, :], wb_ref[...])

        _row_slabs(tm, slab, first, unroll=True)

    @pl.when(k == 1)
    def _():
        def second(rows):
            y = ALPHA * x_ref[rows, :] + o_ref[rows, :] + _dot(b_ref[rows, :], wb_ref[...])
            o_ref[rows, :] = _layer_norm(y, g_ref[...], beta_ref[...])

        _row_slabs(tm, slab, second, unroll=True)


def _outproj_ln(a, b, w_all, layer, x, g_all, beta_all, ln_idx, tm):
    m, d = x.shape
    kh = a.shape[1]
    slab = min(tm, 128)
    vmem = 4 * tm * kh * 2 + 2 * kh * d * 4 + kh * d * 2 + 4 * tm * d * 4 + 4 * slab * d * 4 + 4 * 2**20
    return pl.pallas_call(
        functools.partial(_outproj_ln_kernel, slab=slab),
        grid=(m // tm, 2),
        in_specs=[pl.BlockSpec((tm, kh), lambda i, k: (i, 0)),
                  pl.BlockSpec((tm, kh), lambda i, k: (i, 0)),
                  pl.BlockSpec((None, kh, d), lambda i, k: (layer, k, 0)),
                  pl.BlockSpec((tm, d), lambda i, k: (i, 0)),
                  pl.BlockSpec((None, 1, d), lambda i, k: (ln_idx, 0, 0)),
                  pl.BlockSpec((None, 1, d), lambda i, k: (ln_idx, 0, 0))],
        out_specs=pl.BlockSpec((tm, d), lambda i, k: (i, 0)),
        out_shape=jax.ShapeDtypeStruct((m, d), F32),
        scratch_shapes=[pltpu.VMEM((kh, d), BF16)],
        compiler_params=_cparams(("parallel", "arbitrary"), vmem),
        name="outproj_ln",
    )(a, b, w_all, x, g_all, beta_all)


def _row_slabs(n_rows, slab, body, unroll=False):
    if unroll:
        for r in range(n_rows // slab):
            body(pl.ds(r * slab, slab))
        return

    def step(r, carry):
        body(pl.ds(pl.multiple_of(r * slab, slab), slab))
        return carry

    lax.fori_loop(0, n_rows // slab, step, 0)


def _ffn_ln_kernel(x_ref, w1_ref, w2_ref, g_ref, beta_ref, o_ref, xb_ref, w1b_ref, w2b_ref, *, slab, mm_slab):
    f = pl.program_id(1)
    tm = x_ref.shape[0]

    @pl.when(f == 0)
    def _():
        def init(rows):
            xb_ref[rows, :] = x_ref[rows, :].astype(BF16)
            o_ref[rows, :] = jnp.zeros((slab, o_ref.shape[1]), F32)

        _row_slabs(tm, slab, init)

    w1b_ref[...] = w1_ref[...].astype(BF16)
    w2b_ref[...] = w2_ref[...].astype(BF16)

    def accumulate(rows):
        h = jnp.maximum(_dot(xb_ref[rows, :], w1b_ref[...]), 0.0)
        o_ref[rows, :] += _dot((h * h).astype(BF16), w2b_ref[...])

    _row_slabs(tm, mm_slab, accumulate, unroll=True)

    @pl.when(f == pl.num_programs(1) - 1)
    def _():
        def finish(rows):
            o_ref[rows, :] = _layer_norm(ALPHA * x_ref[rows, :] + o_ref[rows, :], g_ref[...], beta_ref[...])

        _row_slabs(tm, slab, finish)


def _ffn_ln(x, w1_all, w2_all, g_all, beta_all, layer, tm, tf):
    m, d = x.shape
    ff = w1_all.shape[2]
    slab = min(tm, 256)
    mm_slab = min(tm, 128)
    vmem = (2 * tm * d * 4 + tm * d * 2 + 4 * d * tf * 4 + 2 * d * tf * 2 + 2 * tm * d * 4
            + mm_slab * tf * 8 + 2 * mm_slab * d * 4 + 4 * slab * d * 4 + 4 * 2**20)
    return pl.pallas_call(
        functools.partial(_ffn_ln_kernel, slab=slab, mm_slab=mm_slab),
        grid=(m // tm, ff // tf),
        in_specs=[pl.BlockSpec((tm, d), lambda i, f: (i, 0)),
                  pl.BlockSpec((None, d, tf), lambda i, f: (layer, 0, f)),
                  pl.BlockSpec((None, tf, d), lambda i, f: (layer, f, 0)),
                  pl.BlockSpec((None, 1, d), lambda i, f: (layer, 0, 0)),
                  pl.BlockSpec((None, 1, d), lambda i, f: (layer, 0, 0))],
        out_specs=pl.BlockSpec((tm, d), lambda i, f: (i, 0)),
        out_shape=jax.ShapeDtypeStruct((m, d), F32),
        scratch_shapes=[pltpu.VMEM((tm, d), BF16), pltpu.VMEM((d, tf), BF16), pltpu.VMEM((tf, d), BF16)],
        compiler_params=_cparams(("parallel", "arbitrary"), vmem),
        name="ffn_ln",
    )(x, w1_all, w2_all, g_all, beta_all)


def _sortable_key(score):
    score = jnp.where(score == 0.0, 0.0, score)
    bits = lax.bitcast_convert_type(score, I32)
    return jnp.where(bits < 0, bits ^ jnp.int32(0x7FFFFFFF), bits)


def _fold_lanes(m):
    acc = m[:, 0:V7X_LANES]
    for c in range(1, m.shape[1] // V7X_LANES):
        acc = acc + m[:, c * V7X_LANES:(c + 1) * V7X_LANES]
    return acc


def _fold_sublanes(m):
    return jnp.sum(m.reshape(m.shape[0] // V7X_SUBLANES, V7X_SUBLANES, m.shape[1]), axis=0)


def _select_topk(load_group, n_groups, n_queries, topk, index_bits, key_axis):
    if key_axis == 1:
        state, acc_shape, fold = (n_queries, 1), (n_queries, V7X_LANES), _fold_lanes
        group_shape = (n_queries, KEY_GROUP)
    else:
        state, acc_shape, fold = (1, n_queries), (V7X_SUBLANES, n_queries), _fold_sublanes
        group_shape = (KEY_GROUP, n_queries)

    def count(pred):
        def body(g, acc):
            return acc + fold(pred(load_group(g), g).astype(I32))

        acc = lax.fori_loop(0, n_groups, body, jnp.zeros(acc_shape, I32))
        return jnp.sum(acc, axis=key_axis, keepdims=True)

    def thr_pass(p, thr):
        cand = thr + lax.shift_left(jnp.int32(1), jnp.int32(31) - p)
        c = count(lambda k, g: k >= cand)
        return jnp.where(c >= topk, cand, thr)

    thr = lax.fori_loop(0, 32, thr_pass, jnp.full(state, INT_MIN, I32))
    c_ge = count(lambda k, g: k >= thr)
    c_gt = count(lambda k, g: k > thr)
    need = topk - c_gt
    has_tie = jnp.max(jnp.where((c_ge > topk) & (thr > INT_MIN), 1, 0)) > 0
    all_idx = jnp.full(state, (1 << index_bits) - 1, I32)

    def tie_path():
        pos = lax.broadcasted_iota(I32, group_shape, key_axis)

        def idx_pass(p, last):
            cand = last + lax.shift_left(jnp.int32(1), jnp.int32(index_bits - 1) - p)
            c = count(lambda k, g: (k == thr) & (pos + g * KEY_GROUP < cand))
            return jnp.where(c <= need - 1, cand, last)

        return lax.fori_loop(0, index_bits, idx_pass, jnp.zeros(state, I32))

    last = lax.cond(has_tie, tie_path, lambda: all_idx)
    return thr, last


def _selected_bias(keys, idx, thr, last):
    sel = ((keys > thr) | ((keys == thr) & (idx <= last))) & (keys > INT_MIN)
    return jnp.where(sel, 0.0, NEG_BIG).astype(F32)


def _stack_heads(qb, group, rows_per_head):
    rep = N_HEADS // N_KV_HEADS
    return jnp.concatenate(
        [qb[:, (group * rep + r) * HEAD_DIM:(group * rep + r + 1) * HEAD_DIM] for r in range(rep)], axis=0)


def _softmax_step(s, v_bf16, m_ref, l_ref, acc_ref, g):
    m_old = m_ref[g]
    m_new = jnp.maximum(m_old, jnp.max(s, axis=1, keepdims=True))
    p = jnp.exp(s - m_new)
    alpha = jnp.exp(m_old - m_new)
    l_ref[g] = alpha * l_ref[g] + jnp.sum(p, axis=1, keepdims=True)
    acc_ref[g] = alpha * acc_ref[g] + _dot(p.astype(BF16), v_bf16)
    m_ref[g] = m_new


def _dsa_prompt_kernel(q_ref, qi_ref, sq_ref, kv_ref, sk_ref, o_ref,
                       kb_ref, vt_ref, kib_ref, keys_ref, m_ref, l_ref, acc_ref, *, topk, index_bits):
    i = pl.program_id(1)
    tq = Q_BLOCK
    rep = N_HEADS // N_KV_HEADS
    seq = kv_ref.shape[0]

    @pl.when(i == 0)
    def _():
        kb_ref[...] = kv_ref[:, 0:KV_W].astype(BF16)
        kib_ref[...] = sk_ref[:, SMALL_KI:SMALL_KI + IDX_DIM].astype(BF16)

        def transpose_v(c, carry):
            k0 = pl.multiple_of(c * KEY_GROUP, KEY_GROUP)
            vt_ref[:, pl.ds(k0, KEY_GROUP)] = kv_ref[pl.ds(k0, KEY_GROUP), KV_W:2 * KV_W].T.astype(BF16)
            return carry

        lax.fori_loop(0, seq // KEY_GROUP, transpose_v, 0)

    n_groups = lax.shift_right_logical(i * tq + tq + KEY_GROUP - 1, int(math.log2(KEY_GROUP)))
    q_pos = i * tq + lax.broadcasted_iota(I32, (KEY_GROUP, tq), 1)
    key_row = lax.broadcasted_iota(I32, (KEY_GROUP, tq), 0)

    qi_t = qi_ref[...].T.astype(BF16)
    qi_all = jnp.concatenate([qi_t[h * IDX_DIM:(h + 1) * IDX_DIM, :] for h in range(N_IDX_HEADS)], axis=1)
    wi_t = sq_ref[...].T[SMALL_WI:SMALL_WI + N_IDX_HEADS, :]

    def score_body(g, carry):
        k0 = pl.multiple_of(g * KEY_GROUP, KEY_GROUP)
        dots = jnp.maximum(_dot(kib_ref[pl.ds(k0, KEY_GROUP), :], qi_all), 0.0)
        sc = jnp.zeros((KEY_GROUP, tq), F32)
        for h in range(N_IDX_HEADS):
            sc = sc + wi_t[h:h + 1, :] * dots[:, h * tq:(h + 1) * tq]
        keys = _sortable_key(sc * IDX_SCALE)
        keys_ref[pl.ds(k0, KEY_GROUP), :] = jnp.where(key_row + k0 <= q_pos, keys, INT_MIN)
        return carry

    lax.fori_loop(0, n_groups, score_body, 0)

    def load_group(g):
        return keys_ref[pl.ds(pl.multiple_of(g * KEY_GROUP, KEY_GROUP), KEY_GROUP), :]

    thr, last = _select_topk(load_group, n_groups, tq, topk, index_bits, key_axis=0)

    q_t = q_ref[...].T.astype(BF16)
    qg_t = [jnp.concatenate([q_t[(g * rep + r) * HEAD_DIM:(g * rep + r + 1) * HEAD_DIM, :] for r in range(rep)], axis=1)
            for g in range(N_KV_HEADS)]
    m_ref[...] = jnp.full(m_ref.shape, NEG_BIG, F32)
    l_ref[...] = jnp.zeros(l_ref.shape, F32)
    acc_ref[...] = jnp.zeros(acc_ref.shape, F32)
    scale = HEAD_DIM ** -0.5

    def attn_body(g, carry):
        k0 = pl.multiple_of(g * KEY_GROUP, KEY_GROUP)
        bias = _selected_bias(keys_ref[pl.ds(k0, KEY_GROUP), :], key_row + k0, thr, last)
        bias = jnp.concatenate([bias] * rep, axis=1)
        for kvh in range(N_KV_HEADS):
            kc = kb_ref[pl.ds(k0, KEY_GROUP), kvh * HEAD_DIM:(kvh + 1) * HEAD_DIM]
            vc_t = vt_ref[kvh * HEAD_DIM:(kvh + 1) * HEAD_DIM, pl.ds(k0, KEY_GROUP)]
            s = _dot(kc, qg_t[kvh]) * scale + bias
            m_old = m_ref[kvh]
            m_new = jnp.maximum(m_old, jnp.max(s, axis=0, keepdims=True))
            p = jnp.exp(s - m_new)
            alpha = jnp.exp(m_old - m_new)
            l_ref[kvh] = alpha * l_ref[kvh] + jnp.sum(p, axis=0, keepdims=True)
            acc_ref[kvh] = alpha * acc_ref[kvh] + _dot(vc_t, p.astype(BF16))
            m_ref[kvh] = m_new
        return carry

    lax.fori_loop(0, n_groups, attn_body, 0)

    for kvh in range(N_KV_HEADS):
        out_t = acc_ref[kvh] / l_ref[kvh]
        for r in range(rep):
            h = kvh * rep + r
            o_ref[:, h * HEAD_DIM:(h + 1) * HEAD_DIM] = out_t[:, r * tq:(r + 1) * tq].T.astype(o_ref.dtype)


def _dsa_prompt(h_main, h_small, batch, seq):
    tq = Q_BLOCK
    nq = seq // tq
    topk = min(TOPK_MAX, seq // 4)
    cols = N_HEADS // N_KV_HEADS * tq
    kv_blk = 2 * KV_W
    kernel = functools.partial(_dsa_prompt_kernel, topk=topk, index_bits=int(math.log2(seq)))
    vmem = (2 * seq * kv_blk * 4 + seq * kv_blk * 2 + 2 * seq * AB_SMALL_W * 4 + seq * V7X_LANES * 2 + tq * seq * 4
            + 2 * tq * (D_ATTN + IDX_W + AB_SMALL_W) * 4 + 2 * tq * D_ATTN * 2
            + N_KV_HEADS * cols * (2 * V7X_SUBLANES + HEAD_DIM) * 4 + 8 * cols * KEY_GROUP * 4 + 4 * 2**20)
    return pl.pallas_call(
        kernel,
        grid=(batch, nq),
        in_specs=[pl.BlockSpec((tq, D_ATTN), lambda b, i: (b * nq + i, 0)),
                  pl.BlockSpec((tq, IDX_W), lambda b, i: (b * nq + i, (D_ATTN + 2 * KV_W) // IDX_W)),
                  pl.BlockSpec((tq, AB_SMALL_W), lambda b, i: (b * nq + i, 0)),
                  pl.BlockSpec((seq, kv_blk), lambda b, i: (b, D_ATTN // kv_blk)),
                  pl.BlockSpec((seq, AB_SMALL_W), lambda b, i: (b, 0))],
        out_specs=pl.BlockSpec((tq, D_ATTN), lambda b, i: (b * nq + i, 0)),
        out_shape=jax.ShapeDtypeStruct((batch * seq, D_ATTN), BF16),
        scratch_shapes=[pltpu.VMEM((seq, KV_W), BF16),
                        pltpu.VMEM((KV_W, seq), BF16),
                        pltpu.VMEM((seq, IDX_DIM), BF16),
                        pltpu.VMEM((seq, tq), I32),
                        pltpu.VMEM((N_KV_HEADS, 1, cols), F32),
                        pltpu.VMEM((N_KV_HEADS, 1, cols), F32),
                        pltpu.VMEM((N_KV_HEADS, HEAD_DIM, cols), F32)],
        compiler_params=_cparams(("parallel", "arbitrary"), vmem),
        name="dsa_prompt",
    )(h_main, h_main, h_small, h_main, h_small)


def _dsa_sample_scores_kernel(pt_ref, qi_ref, sq_ref, *refs, pages_per_step):
    page_refs = refs[:pages_per_step]
    o_ref = refs[pages_per_step]
    n = qi_ref.shape[0]
    qib = qi_ref[...].astype(BF16)
    qs = jnp.concatenate([qib[:, h * IDX_DIM:(h + 1) * IDX_DIM] for h in range(N_IDX_HEADS)], axis=0)
    wi = sq_ref[:, SMALL_WI:SMALL_WI + N_IDX_HEADS]
    for t in range(pages_per_step):
        dots = jnp.maximum(_dot(qs, page_refs[t][...].astype(BF16)), 0.0)
        sc = jnp.zeros((n, PAGE_SIZE), F32)
        for h in range(N_IDX_HEADS):
            sc = sc + wi[:, h:h + 1] * dots[h * n:(h + 1) * n, :]
        o_ref[:, t * PAGE_SIZE:(t + 1) * PAGE_SIZE] = sc * IDX_SCALE


def _dsa_sample_scores(page_table, h_main, h_small, cache_ki, layer, n, pages_per_step):
    batch, n_pages = page_table.shape
    steps = n_pages // pages_per_step
    kernel = functools.partial(_dsa_sample_scores_kernel, pages_per_step=pages_per_step)

    def page_spec(t):
        return pl.BlockSpec((None, None, IDX_DIM, PAGE_SIZE),
                            lambda b, s, pt: (layer, pt[b, s * pages_per_step + t], 0, 0))

    grid_spec = pltpu.PrefetchScalarGridSpec(
        num_scalar_prefetch=1,
        grid=(batch, steps),
        in_specs=[pl.BlockSpec((n, IDX_W), lambda b, s, pt: (b, (D_ATTN + 2 * KV_W) // IDX_W)),
                  pl.BlockSpec((n, AB_SMALL_W), lambda b, s, pt: (b, 0))]
                 + [page_spec(t) for t in range(pages_per_step)],
        out_specs=pl.BlockSpec((None, n, pages_per_step * PAGE_SIZE), lambda b, s, pt: (b, 0, s)),
    )
    vmem = 4 * pages_per_step * PAGE_SIZE * V7X_LANES * 4 + 4 * n * pages_per_step * PAGE_SIZE * 4 + 8 * 2**20
    return pl.pallas_call(
        kernel,
        grid_spec=grid_spec,
        out_shape=jax.ShapeDtypeStruct((batch, n, n_pages * PAGE_SIZE), F32),
        compiler_params=_cparams(("parallel", "arbitrary"), vmem),
        name="dsa_sample_scores",
    )(page_table, h_main, h_small, *([cache_ki] * pages_per_step))


def _dsa_sample_attend_kernel(pt_ref, sc_ref, q_ref, qi_ref, sq_ref, kvn_ref, *refs,
                              pages_per_step, topk, index_bits, past):
    k_refs = refs[:pages_per_step]
    v_refs = refs[pages_per_step:2 * pages_per_step]
    o_ref = refs[2 * pages_per_step]
    keys_ref, thr_ref, last_ref, m_ref, l_ref, acc_ref, pad_ref = refs[2 * pages_per_step + 1:]
    s_idx = pl.program_id(1)
    n = q_ref.shape[0]
    rep = N_HEADS // N_KV_HEADS
    n_groups = keys_ref.shape[1] // KEY_GROUP
    lane_p = lax.broadcasted_iota(I32, (n, PAGE_SIZE), 1)
    scale = HEAD_DIM ** -0.5

    @pl.when(s_idx == 0)
    def _():
        keys_ref[...] = jnp.full(keys_ref.shape, INT_MIN, I32)
        keys_ref[:, 0:past] = _sortable_key(sc_ref[...])
        qib = qi_ref[...].astype(BF16)
        wi = sq_ref[:, SMALL_WI:SMALL_WI + N_IDX_HEADS]
        pad_ref[...] = jnp.zeros(pad_ref.shape, F32)
        pad_ref[0:n, 0:IDX_DIM] = sq_ref[:, SMALL_KI:SMALL_KI + IDX_DIM]
        ki_new = pad_ref[:, 0:IDX_DIM].astype(BF16)
        sc = jnp.zeros((n, PAGE_SIZE), F32)
        for h in range(N_IDX_HEADS):
            dots = _dot_nt(qib[:, h * IDX_DIM:(h + 1) * IDX_DIM], ki_new)
            sc = sc + wi[:, h:h + 1] * jnp.maximum(dots, 0.0)
        q_row = lax.broadcasted_iota(I32, (n, PAGE_SIZE), 0)
        keys_ref[:, past:past + PAGE_SIZE] = jnp.where(lane_p <= q_row, _sortable_key(sc * IDX_SCALE), INT_MIN)

        def load_group(g):
            return keys_ref[:, pl.ds(pl.multiple_of(g * KEY_GROUP, KEY_GROUP), KEY_GROUP)]

        thr, last = _select_topk(load_group, n_groups, n, topk, index_bits, key_axis=1)
        thr_ref[...] = thr
        last_ref[...] = last
        m_ref[...] = jnp.full(m_ref.shape, NEG_BIG, F32)
        l_ref[...] = jnp.zeros(l_ref.shape, F32)
        acc_ref[...] = jnp.zeros(acc_ref.shape, F32)

    thr = thr_ref[...]
    last = last_ref[...]
    qb = q_ref[...].astype(BF16)
    qg = [_stack_heads(qb, g, n) for g in range(N_KV_HEADS)]

    def bias_at(k0):
        keys = keys_ref[:, pl.ds(k0, PAGE_SIZE)]
        b = _selected_bias(keys, lane_p + k0, thr, last)
        return jnp.concatenate([b] * rep, axis=0)

    base = s_idx * (pages_per_step * PAGE_SIZE)
    bias = jnp.concatenate([bias_at(pl.multiple_of(base + t * PAGE_SIZE, PAGE_SIZE)) for t in range(pages_per_step)], axis=1)
    for kvh in range(N_KV_HEADS):
        head_rows = pl.ds(kvh, PAGE_SIZE, stride=N_KV_HEADS)
        kc = jnp.concatenate([k_refs[t][head_rows, :].astype(BF16) for t in range(pages_per_step)], axis=0)
        vc = jnp.concatenate([v_refs[t][head_rows, :].astype(BF16) for t in range(pages_per_step)], axis=0)
        s = _dot_nt(qg[kvh], kc) * scale + bias
        _softmax_step(s, vc, m_ref, l_ref, acc_ref, kvh)

    @pl.when(s_idx == pl.num_programs(1) - 1)
    def _():
        bias_n = bias_at(past)
        for kvh in range(N_KV_HEADS):
            pad_ref[0:n, :] = kvn_ref[:, kvh * HEAD_DIM:(kvh + 1) * HEAD_DIM]
            kc = pad_ref[...].astype(BF16)
            pad_ref[0:n, :] = kvn_ref[:, KV_W + kvh * HEAD_DIM:KV_W + (kvh + 1) * HEAD_DIM]
            vc = pad_ref[...].astype(BF16)
            s = _dot_nt(qg[kvh], kc) * scale + bias_n
            _softmax_step(s, vc, m_ref, l_ref, acc_ref, kvh)
        for kvh in range(N_KV_HEADS):
            out = acc_ref[kvh] / l_ref[kvh]
            for r in range(rep):
                h = kvh * rep + r
                o_ref[:, h * HEAD_DIM:(h + 1) * HEAD_DIM] = out[r * n:(r + 1) * n, :].astype(o_ref.dtype)


def _dsa_sample_attend(page_table, scores, h_main, h_small, cache_k, cache_v, layer, n, pages_per_step):
    batch, n_pages = page_table.shape
    past = n_pages * PAGE_SIZE
    steps = n_pages // pages_per_step
    topk = min(TOPK_MAX, (past + n) // 4)
    total = past + PAGE_SIZE
    padded = -(-total // KEY_GROUP) * KEY_GROUP
    index_bits = int(math.ceil(math.log2(padded)))
    rows = N_HEADS // N_KV_HEADS * n
    kernel = functools.partial(_dsa_sample_attend_kernel, pages_per_step=pages_per_step, topk=topk,
                               index_bits=index_bits, past=past)

    def page_spec(t):
        return pl.BlockSpec((None, None, PAGE_SIZE * N_KV_HEADS, HEAD_DIM),
                            lambda b, s, pt: (layer, pt[b, s * pages_per_step + t], 0, 0))

    kv_blk = 2 * KV_W
    grid_spec = pltpu.PrefetchScalarGridSpec(
        num_scalar_prefetch=1,
        grid=(batch, steps),
        in_specs=[pl.BlockSpec((None, n, past), lambda b, s, pt: (b, 0, 0)),
                  pl.BlockSpec((n, D_ATTN), lambda b, s, pt: (b, 0)),
                  pl.BlockSpec((n, IDX_W), lambda b, s, pt: (b, (D_ATTN + 2 * KV_W) // IDX_W)),
                  pl.BlockSpec((n, AB_SMALL_W), lambda b, s, pt: (b, 0)),
                  pl.BlockSpec((n, kv_blk), lambda b, s, pt: (b, D_ATTN // kv_blk))]
                 + [page_spec(t) for t in range(pages_per_step)] * 2,
        out_specs=pl.BlockSpec((n, D_ATTN), lambda b, s, pt: (b, 0)),
        scratch_shapes=[pltpu.VMEM((n, padded), I32),
                        pltpu.VMEM((n, 1), I32),
                        pltpu.VMEM((n, 1), I32),
                        pltpu.VMEM((N_KV_HEADS, rows, 1), F32),
                        pltpu.VMEM((N_KV_HEADS, rows, 1), F32),
                        pltpu.VMEM((N_KV_HEADS, rows, HEAD_DIM), F32),
                        pltpu.VMEM((PAGE_SIZE, HEAD_DIM), F32)],
    )
    vmem = (4 * 2 * pages_per_step * PAGE_SIZE * KV_W * 4 + 4 * n * padded * 4
            + 8 * rows * pages_per_step * PAGE_SIZE * 4 + 8 * 2**20)
    return pl.pallas_call(
        kernel,
        grid_spec=grid_spec,
        out_shape=jax.ShapeDtypeStruct((batch * n, D_ATTN), BF16),
        compiler_params=_cparams(("parallel", "arbitrary"), vmem),
        name="dsa_sample_attend",
    )(page_table, scores, h_main, h_main, h_small, h_main, *([cache_k] * pages_per_step), *([cache_v] * pages_per_step))


def _ssd_kernel(z_ref, xbc_ref, dtc_ref, dtr_ref, pre_ref, h0_ref, cw_ref, cb_ref, dtb_ref, alog_ref, dsk_ref,
                dtb_c_ref, alog_c_ref, g_ref, y_ref, hl_ref, state_ref, stage_ref, dtpad_ref, zpad_ref, *, t_in):
    c = pl.program_id(1)
    tt = SSD_CHUNK
    hpg = SSD_HEADS // SSD_GROUPS
    gw = hpg * SSD_HEAD_DIM

    @pl.when(c == 0)
    def _():
        state_ref[...] = h0_ref[...]
        stage_ref[0:V7X_SUBLANES, :] = pre_ref[...]
        if t_in < tt:
            stage_ref[V7X_SUBLANES:, :] = jnp.zeros((tt, CONV_DIM), F32)
            dtpad_ref[...] = jnp.zeros(dtpad_ref.shape, F32)
            zpad_ref[...] = jnp.zeros(zpad_ref.shape, F32)

    stage_ref[V7X_SUBLANES:V7X_SUBLANES + t_in, :] = xbc_ref[...]
    conv = cb_ref[...] + jnp.zeros((tt, CONV_DIM), F32)
    for j in range(CONV_W):
        off = V7X_SUBLANES - (CONV_W - 1) + j
        conv = conv + cw_ref[j:j + 1, :] * stage_ref[off:off + tt, :]
    if t_in == tt:
        stage_ref[0:V7X_SUBLANES, :] = xbc_ref[tt - V7X_SUBLANES:tt, :]
    xbc = conv * jax.nn.sigmoid(conv)
    xs = xbc[:, 0:D_SSD]

    row_valid = lax.broadcasted_iota(I32, (tt, 1), 0) < t_in
    lane_valid = lax.broadcasted_iota(I32, (1, tt), 1) < t_in
    if t_in < tt:
        dtpad_ref[0:t_in, :] = dtc_ref[...]
        dt_raw_c = dtpad_ref[...]
        dtpad_ref[0:SSD_HEADS, 0:t_in] = dtr_ref[...]
        dt_raw_r = dtpad_ref[0:SSD_HEADS, :]
        zpad_ref[0:t_in, :] = z_ref[...]
        z = zpad_ref[...]
    else:
        dt_raw_c = dtc_ref[...]
        dt_raw_r = dtr_ref[...]
        z = z_ref[...]

    hl = lax.broadcasted_iota(I32, (1, AB_SMALL_W), 1)
    head_lane = (hl >= SMALL_DT) & (hl < SMALL_DT + SSD_HEADS)
    dt_c = jnp.where(row_valid & head_lane, jax.nn.softplus(dt_raw_c + dtb_ref[...]), 0.0)
    dt_r = jnp.where(lane_valid, jax.nn.softplus(dt_raw_r + dtb_c_ref[...]), 0.0)
    a_c = -jnp.exp(alog_ref[...])
    a_r = -jnp.exp(alog_c_ref[...])
    ri = lax.broadcasted_iota(I32, (tt, tt), 0)
    ci = lax.broadcasted_iota(I32, (tt, tt), 1)
    tril = ri >= ci
    acum_c = _dot_exact(jnp.where(tril, 1.0, 0.0).astype(F32), dt_c * a_c)
    acum_r = _dot_exact(dt_r * a_r, jnp.where(ri <= ci, 1.0, 0.0).astype(F32))
    acum_last = acum_c[tt - 1:tt, :]

    eh = lax.broadcasted_iota(I32, (AB_SMALL_W, D_SSD), 0) - SMALL_DT
    ef = lax.broadcasted_iota(I32, (AB_SMALL_W, D_SSD), 1)
    expand = jnp.where(lax.shift_right_logical(ef, int(math.log2(SSD_HEAD_DIM))) == eh, 1.0, 0.0).astype(F32)
    decay_in = _dot_exact(jnp.exp(acum_c), expand)
    decay_out = _dot_exact(jnp.exp(acum_last - acum_c) * dt_c, expand)
    chunk_decay = _dot_exact(jnp.exp(acum_last) + jnp.zeros((V7X_SUBLANES, 1), F32), expand)[0:1, :]
    d_skip = _dot_exact(dsk_ref[...] + jnp.zeros((V7X_SUBLANES, 1), F32), expand)[0:1, :]

    xs_b = xs.astype(BF16)
    xw_b = (xs * decay_out).astype(BF16)
    lane128 = lax.broadcasted_iota(I32, (tt, 2 * SSD_HEAD_DIM), 1)
    y_parts = []
    new_state = []
    for g in range(SSD_GROUPS):
        bm = xbc[:, D_SSD + g * SSD_STATE:D_SSD + (g + 1) * SSD_STATE].astype(BF16)
        cm = xbc[:, D_SSD + (SSD_GROUPS + g) * SSD_STATE:D_SSD + (SSD_GROUPS + g + 1) * SSD_STATE].astype(BF16)
        cb = _dot_nt(cm, bm)
        st_g = state_ref[:, g * gw:(g + 1) * gw]
        y_off = _dot(cm, st_g.astype(BF16)) * decay_in[:, g * gw:(g + 1) * gw]
        new_state.append(st_g * chunk_decay[:, g * gw:(g + 1) * gw] + _dot_tn(bm, xw_b[:, g * gw:(g + 1) * gw]))
        for pair in range(hpg // 2):
            ms = []
            for hh in range(2):
                h = g * hpg + 2 * pair + hh
                seg = jnp.where(tril, acum_c[:, SMALL_DT + h:SMALL_DT + h + 1] - acum_r[h:h + 1, :], -jnp.inf)
                ms.append((cb * jnp.exp(seg) * dt_r[h:h + 1, :]).astype(BF16))
            col = (g * hpg + 2 * pair) * SSD_HEAD_DIM
            xp = xs_b[:, col:col + 2 * SSD_HEAD_DIM]
            zero = jnp.zeros_like(xp)
            rhs = jnp.concatenate([jnp.where(lane128 < SSD_HEAD_DIM, xp, zero),
                                   jnp.where(lane128 >= SSD_HEAD_DIM, xp, zero)], axis=0)
            y_parts.append(_dot(jnp.concatenate(ms, axis=1), rhs) + y_off[:, col - g * gw:col - g * gw + 2 * SSD_HEAD_DIM])
    for g in range(SSD_GROUPS):
        state_ref[:, g * gw:(g + 1) * gw] = new_state[g]

    y = jnp.concatenate(y_parts, axis=1) + d_skip * xs
    y = y * (z * jax.nn.sigmoid(z))
    y = y * lax.rsqrt(jnp.mean(y * y, axis=-1, keepdims=True) + EPS) * g_ref[...]
    y_ref[...] = y[0:t_in, :].astype(y_ref.dtype)

    @pl.when(c == pl.num_programs(1) - 1)
    def _():
        hl_ref[...] = state_ref[...]


def _ssd(h_main, h_small, dt_rows, prefix, h0, conv_w, conv_b, dt_bias, a_log, d_skip, ssd_g, batch, n_chunks, t_in):
    tt = SSD_CHUNK
    kernel = functools.partial(_ssd_kernel, t_in=t_in)
    z_blk = (D_ATTN + 2 * KV_W + IDX_W) // D_SSD
    xbc_blk = (D_ATTN + 2 * KV_W + IDX_W + D_SSD) // CONV_DIM
    row = lambda v: v.reshape(1, -1)
    col = lambda v: v.reshape(-1, 1)
    head_row = lambda v: jnp.pad(v.reshape(1, -1), ((0, 0), (SMALL_DT, AB_SMALL_W - SMALL_DT - SSD_HEADS)))
    full = lambda shape: pl.BlockSpec(shape, lambda b, c: (0,) * len(shape))
    vmem = 40 * tt * CONV_DIM * 4 + 8 * SSD_STATE * D_SSD * 4 + 8 * 2**20
    return pl.pallas_call(
        kernel,
        grid=(batch, n_chunks),
        in_specs=[pl.BlockSpec((t_in, D_SSD), lambda b, c: (b * n_chunks + c, z_blk)),
                  pl.BlockSpec((t_in, CONV_DIM), lambda b, c: (b * n_chunks + c, xbc_blk)),
                  pl.BlockSpec((t_in, AB_SMALL_W), lambda b, c: (b * n_chunks + c, 0)),
                  pl.BlockSpec((None, SSD_HEADS, t_in), lambda b, c: (b * n_chunks + c, 0, 0)),
                  pl.BlockSpec((None, V7X_SUBLANES, CONV_DIM), lambda b, c: (b, 0, 0)),
                  pl.BlockSpec((None, SSD_STATE, D_SSD), lambda b, c: (b, 0, 0)),
                  full((CONV_W, CONV_DIM)), full((1, CONV_DIM)), full((1, AB_SMALL_W)), full((1, AB_SMALL_W)),
                  full((1, AB_SMALL_W)), full((SSD_HEADS, 1)), full((SSD_HEADS, 1)), full((1, D_SSD))],
        out_specs=[pl.BlockSpec((t_in, D_SSD), lambda b, c: (b * n_chunks + c, 0)),
                   pl.BlockSpec((None, SSD_STATE, D_SSD), lambda b, c: (b, 0, 0))],
        out_shape=[jax.ShapeDtypeStruct((batch * n_chunks * t_in, D_SSD), BF16),
                   jax.ShapeDtypeStruct((batch, SSD_STATE, D_SSD), F32)],
        scratch_shapes=[pltpu.VMEM((SSD_STATE, D_SSD), F32),
                        pltpu.VMEM((V7X_SUBLANES + tt, CONV_DIM), F32),
                        pltpu.VMEM((tt, AB_SMALL_W), F32),
                        pltpu.VMEM((tt, D_SSD), F32)],
        compiler_params=_cparams(("parallel", "arbitrary"), vmem),
        name="ssd",
    )(h_main, h_main, h_small, dt_rows, prefix, h0, conv_w, row(conv_b), head_row(dt_bias), head_row(a_log), head_row(d_skip),
      col(dt_bias), col(a_log), row(ssd_g))


def _cd_kernel(p_ref, u_ref, v_ref, pre_ref, pw_ref, ps_ref, lg_ref, lb_ref, sw_ref, sb_ref,
               pooled_ref, gated_ref, vout_ref, stage_ref, upad_ref, vpad_ref, *, t_in, prefix_len):
    c = pl.program_id(1)
    tt = SGU_CHUNK
    hist = max(POOL_WINDOWS)

    @pl.when(c == 0)
    def _():
        stage_ref[0:hist, :] = pre_ref[...]
        if t_in < tt:
            stage_ref[hist:, :] = jnp.zeros((tt, POOL_W), F32)
            upad_ref[...] = jnp.zeros(upad_ref.shape, F32)
            vpad_ref[...] = jnp.zeros(vpad_ref.shape, F32)

    stage_ref[hist:hist + t_in, :] = p_ref[...]
    end = prefix_len + c * tt + lax.broadcasted_iota(I32, (tt, 1), 0) + 1
    pooled = []
    for j, w in enumerate(POOL_WINDOWS):
        cols = slice(j * POOL_GW, (j + 1) * POOL_GW)
        cur = stage_ref[hist:hist + tt, cols]
        tot = cur
        for k in range(1, w):
            tot = tot + stage_ref[hist - k:hist - k + tt, cols]
        cnt = jnp.minimum(end, w).astype(F32)
        d = (tot / cnt - cur).astype(BF16)
        pooled.append(_dot(d, pw_ref[j].astype(BF16)))
    if t_in == tt:
        stage_ref[0:hist, :] = p_ref[tt - hist:tt, :]
    pooled = jnp.concatenate(pooled, axis=1) * ps_ref[...]
    pooled_ref[...] = pooled[0:t_in, :].astype(pooled_ref.dtype)

    gelu = lambda x: x * (lax.erf(x / math.sqrt(2.0)) + 1.0) / 2.0
    u = gelu(u_ref[...])
    v = _layer_norm(gelu(v_ref[...]), lg_ref[...], lb_ref[...])
    vout_ref[...] = v
    if t_in < tt:
        upad_ref[0:t_in, :] = u
        vpad_ref[0:t_in, :] = v
        u = upad_ref[...]
        v = vpad_ref[...]
    ri = lax.broadcasted_iota(I32, (tt, tt), 0)
    ci = lax.broadcasted_iota(I32, (tt, tt), 1)
    vb = v.astype(BF16)
    gates = []
    for g in range(SGU_GROUPS):
        wm = jnp.where(ri >= ci, sw_ref[g], 0.0).astype(BF16)
        gates.append(_dot(wm, vb[:, g * SGU_GW:(g + 1) * SGU_GW]) + sb_ref[:, g:g + 1])
    gated = u * jnp.concatenate(gates, axis=1)
    gated_ref[...] = gated[0:t_in, :].astype(gated_ref.dtype)


def _cd_mixer(h, prefix, pool_w, pool_scale, ln_g, ln_b, sgu_w, sgu_b, batch, n_chunks, t_in, prefix_len):
    tt = SGU_CHUNK
    hist = max(POOL_WINDOWS)
    kernel = functools.partial(_cd_kernel, t_in=t_in, prefix_len=prefix_len)
    row = lambda v: v.reshape(1, -1)
    full = lambda shape: pl.BlockSpec(shape, lambda b, c: (0,) * len(shape))
    m = batch * n_chunks * t_in
    vmem = 40 * tt * POOL_W * 4 + 4 * POOL_GW * POOL_GW * len(POOL_WINDOWS) * 4 + 8 * 2**20
    return pl.pallas_call(
        kernel,
        grid=(batch, n_chunks),
        in_specs=[pl.BlockSpec((t_in, POOL_W), lambda b, c: (b * n_chunks + c, 0)),
                  pl.BlockSpec((t_in, SGU_W), lambda b, c: (b * n_chunks + c, 1)),
                  pl.BlockSpec((t_in, SGU_W), lambda b, c: (b * n_chunks + c, 2)),
                  pl.BlockSpec((None, hist, POOL_W), lambda b, c: (b, 0, 0)),
                  full((len(POOL_WINDOWS), POOL_GW, POOL_GW)), full((1, POOL_W)), full((1, SGU_W)), full((1, SGU_W)),
                  full((SGU_GROUPS, tt, tt)), full((tt, SGU_GROUPS))],
        out_specs=[pl.BlockSpec((t_in, POOL_W), lambda b, c: (b * n_chunks + c, 0)),
                   pl.BlockSpec((t_in, SGU_W), lambda b, c: (b * n_chunks + c, 0)),
                   pl.BlockSpec((t_in, SGU_W), lambda b, c: (b * n_chunks + c, 0))],
        out_shape=[jax.ShapeDtypeStruct((m, POOL_W), BF16),
                   jax.ShapeDtypeStruct((m, SGU_W), BF16),
                   jax.ShapeDtypeStruct((m, SGU_W), F32)],
        scratch_shapes=[pltpu.VMEM((hist + tt, POOL_W), F32),
                        pltpu.VMEM((tt, SGU_W), F32),
                        pltpu.VMEM((tt, SGU_W), F32)],
        compiler_params=_cparams(("parallel", "arbitrary"), vmem),
        name="cd_mixer",
    )(h, h, h, prefix, pool_w, row(pool_scale), row(ln_g), row(ln_b), sgu_w, sgu_b.T)


def _row_tile(m, cap):
    t = min(m, cap)
    while m % t:
        t //= 2
    return t


def _ab_layer(yp, ys, j, i, bp, sp, bs, ns, cache_k, cache_v, cache_idx_k, state_ssm, state_conv, page_table,
              w_in_ab, conv_w, conv_b, dt_bias, a_log, d_skip, ssd_norm_g, w_out_ab, ln_g, ln_b):
    o = [0]
    for s in AB_SIZES:
        o.append(o[-1] + s)
    w_rest = jnp.concatenate([w_in_ab[j, :, o[6]:o[8]], w_in_ab[j, :, o[4]:o[6]], w_in_ab[j, :, o[8]:o[9]],
                              jnp.zeros((D_MODEL, AB_SMALL_W - (o[6] - o[4]) - (o[9] - o[8])), w_in_ab.dtype)], axis=1)
    outs = []
    states = []
    for x, batch, n in ((yp, bp, sp), (ys, bs, ns)):
        m = batch * n
        hm, hs = _inproj_ab(x, w_in_ab, w_rest, j, _row_tile(m, 1024), 512)
        prompt = n % SSD_CHUNK == 0
        t_in = SSD_CHUNK if prompt else n
        n_chunks = n // t_in
        dt_rows = hs[:, SMALL_DT:SMALL_DT + SSD_HEADS].reshape(batch * n_chunks, t_in, SSD_HEADS).transpose(0, 2, 1)
        if prompt:
            attn = _dsa_prompt(hm, hs, batch, n)
            prefix = jnp.zeros((batch, V7X_SUBLANES, CONV_DIM), F32)
            h0 = jnp.zeros((batch, SSD_STATE, D_SSD), F32)
        else:
            pages = 16
            page_view = cache_k.shape[:2] + (PAGE_SIZE * N_KV_HEADS, HEAD_DIM)
            scores = _dsa_sample_scores(page_table, hm, hs, jnp.swapaxes(cache_idx_k, 2, 3), j, n, pages)
            attn = _dsa_sample_attend(page_table, scores, hm, hs, cache_k.reshape(page_view), cache_v.reshape(page_view),
                                      j, n, pages)
            prefix = jnp.concatenate([jnp.zeros((batch, V7X_SUBLANES - (CONV_W - 1), CONV_DIM), F32), state_conv[j]], axis=1)
            h0 = state_ssm[j].transpose(0, 3, 1, 2).reshape(batch, SSD_STATE, D_SSD)
        y_ssd, h_last = _ssd(hm, hs, dt_rows, prefix, h0, conv_w[j], conv_b[j], dt_bias[j], a_log[j], d_skip[j],
                             ssd_norm_g[j], batch, n_chunks, t_in)
        outs.append(_outproj_ln(attn, y_ssd, w_out_ab, j, x, ln_g, ln_b, i, _row_tile(m, 512)))
        k = hm[:, o[1]:o[2]].reshape(batch, n, N_KV_HEADS, HEAD_DIM)
        v = hm[:, o[2]:o[3]].reshape(batch, n, N_KV_HEADS, HEAD_DIM)
        ki = hs[:, SMALL_KI:SMALL_KI + IDX_DIM].reshape(batch, n, IDX_DIM)
        ssm = h_last.reshape(batch, SSD_STATE, SSD_HEADS, SSD_HEAD_DIM).transpose(0, 2, 3, 1)
        xbc_off = o[4] + D_SSD
        conv_new = hm[:, xbc_off:xbc_off + CONV_DIM].reshape(batch, n, CONV_DIM)[:, n - (CONV_W - 1):]
        states.append((k, v, ki, ssm, conv_new))
    return outs, states


def _cd_layer(yp, ys, j, i, bp, sp, bs, ns, state_pool, w_in_cd, pool_w, pool_scale, sgu_ln_g, sgu_ln_b, sgu_w, sgu_b,
              w_out_cd, ln_g, ln_b):
    hist = max(POOL_WINDOWS)
    outs = []
    states = []
    for x, batch, n in ((yp, bp, sp), (ys, bs, ns)):
        m = batch * n
        h = _inproj(x, w_in_cd, j, _row_tile(m, 1024), 512)
        prompt = n % SGU_CHUNK == 0
        t_in = SGU_CHUNK if prompt else n
        n_chunks = n // t_in
        p = h[:, 0:POOL_W].reshape(batch, n, POOL_W)
        if prompt:
            prefix = jnp.zeros((batch, hist, POOL_W), F32)
            prefix_len = 0
            pool_state = p[:, n - POOL_STATE:]
        else:
            prefix = jnp.concatenate([jnp.zeros((batch, hist - POOL_STATE, POOL_W), F32), state_pool[j]], axis=1)
            prefix_len = POOL_STATE
            pool_state = jnp.concatenate([state_pool[j], p], axis=1)[:, n:]
        pooled, gated, v = _cd_mixer(h, prefix, pool_w[j], pool_scale[j], sgu_ln_g[j], sgu_ln_b[j], sgu_w[j], sgu_b[j],
                                     batch, n_chunks, t_in, prefix_len)
        outs.append(_outproj_ln(pooled, gated, w_out_cd, j, x, ln_g, ln_b, i, _row_tile(m, 512)))
        states.append((pool_state, v.reshape(batch, n, SGU_W)))
    return outs, states


def kernel(x_prompt, x_sample, cache_k, cache_v, cache_idx_k, state_ssm, state_conv, state_pool, page_table, ln1_g, ln1_b, ln2_g, ln2_b, w_in_ab, conv_w, conv_b, dt_bias, a_log, d_skip, ssd_norm_g, w_out_ab, w_in_cd, pool_w, pool_scale, sgu_ln_g, sgu_ln_b, sgu_w, sgu_b, w_out_cd, w_ff1, w_ff2):
    bp, sp, d = x_prompt.shape
    bs, ns, _ = x_sample.shape
    depth = w_ff1.shape[0]
    yp = x_prompt.reshape(bp * sp, d)
    ys = x_sample.reshape(bs * ns, d)
    ab_p, ab_s, cd_p, cd_s = [], [], [], []
    ln1 = (ln1_g.reshape(depth, 1, d), ln1_b.reshape(depth, 1, d))
    ln2 = (ln2_g.reshape(depth, 1, d), ln2_b.reshape(depth, 1, d))
    for i in range(depth):
        j = i // 2
        if i % 2 == 0:
            (yp, ys), (st_p, st_s) = _ab_layer(yp, ys, j, i, bp, sp, bs, ns, cache_k, cache_v, cache_idx_k, state_ssm,
                                               state_conv, page_table, w_in_ab, conv_w, conv_b, dt_bias, a_log, d_skip,
                                               ssd_norm_g, w_out_ab, *ln1)
            ab_p.append(st_p)
            ab_s.append(st_s)
        else:
            (yp, ys), (st_p, st_s) = _cd_layer(yp, ys, j, i, bp, sp, bs, ns, state_pool, w_in_cd, pool_w, pool_scale,
                                               sgu_ln_g, sgu_ln_b, sgu_w, sgu_b, w_out_cd, *ln1)
            cd_p.append(st_p)
            cd_s.append(st_s)
        yp = _ffn_ln(yp, w_ff1, w_ff2, *ln2, i, _row_tile(bp * sp, 1024), 256)
        ys = _ffn_ln(ys, w_ff1, w_ff2, *ln2, i, _row_tile(bs * ns, 1024), 256)
    stack = lambda lst, idx: jnp.stack([t[idx] for t in lst])
    return (yp.reshape(bp, sp, d), ys.reshape(bs, ns, d),
            stack(ab_p, 0), stack(ab_p, 1), stack(ab_p, 2), stack(ab_p, 3), stack(ab_p, 4), stack(cd_p, 0),
            stack(ab_s, 0), stack(ab_s, 1), stack(ab_s, 2), stack(ab_s, 3), stack(ab_s, 4), stack(cd_s, 0), stack(cd_s, 1))
```

```python
import functools
import math

import jax
import jax.numpy as jnp
from jax import lax
from jax.experimental import pallas as pl
from jax.experimental.pallas import tpu as pltpu

D_MODEL = 2048
PAGE_SIZE = 128
N_HEADS = 8
N_KV_HEADS = 2
HEAD_DIM = 128
D_ATTN = N_HEADS * HEAD_DIM
KV_W = N_KV_HEADS * HEAD_DIM
N_IDX_HEADS = 8
IDX_DIM = 64
IDX_W = N_IDX_HEADS * IDX_DIM
IDX_SCALE = (N_IDX_HEADS * IDX_DIM) ** -0.5
TOPK_MAX = 256
Q_BLOCK = 128
SSD_HEADS = 16
SSD_HEAD_DIM = 64
D_SSD = SSD_HEADS * SSD_HEAD_DIM
SSD_GROUPS = 2
SSD_STATE = 128
CONV_W = 4
CONV_DIM = D_SSD + 2 * SSD_GROUPS * SSD_STATE
SSD_CHUNK = 128
POOL_WINDOWS = (2, 4, 8, 16)
POOL_W = D_MODEL // 2
POOL_GW = POOL_W // len(POOL_WINDOWS)
POOL_STATE = max(POOL_WINDOWS) - 1
SGU_W = D_MODEL // 2
SGU_GROUPS = 4
SGU_GW = SGU_W // SGU_GROUPS
SGU_CHUNK = 128
DEPTH = 4
ALPHA = (2 * DEPTH) ** 0.25
EPS = 1e-5
AB_SIZES = (D_ATTN, KV_W, KV_W, IDX_W, IDX_DIM, N_IDX_HEADS, D_SSD, CONV_DIM, SSD_HEADS)

AB_MAIN_W = D_ATTN + 2 * KV_W + IDX_W + D_SSD + CONV_DIM
AB_SMALL_W = 128
SMALL_KI = 0
SMALL_WI = IDX_DIM
SMALL_DT = IDX_DIM + N_IDX_HEADS

V7X_LANES = 128
V7X_SUBLANES = 8
V7X_VMEM_BYTES = 64 * 2**20

KEY_GROUP = 512
INT_MIN = -2**31
NEG_BIG = -1e30

F32 = jnp.float32
BF16 = jnp.bfloat16
I32 = jnp.int32


def _cparams(sem, vmem_bytes):
    return pltpu.CompilerParams(dimension_semantics=sem, vmem_limit_bytes=int(min(vmem_bytes, V7X_VMEM_BYTES - 4 * 2**20)))


def _dot(a, b):
    return jnp.dot(a, b, preferred_element_type=F32)


def _dot_nt(a, b):
    return lax.dot_general(a, b, (((1,), (1,)), ((), ())), preferred_element_type=F32)


def _dot_tn(a, b):
    return lax.dot_general(a, b, (((0,), (0,)), ((), ())), preferred_element_type=F32)


def _dot_exact(a, b):
    return jnp.dot(a, b, preferred_element_type=F32, precision=lax.Precision.HIGHEST)


def _layer_norm(x, g, b):
    mu = jnp.mean(x, axis=-1, keepdims=True)
    xc = x - mu
    var = jnp.mean(xc * xc, axis=-1, keepdims=True)
    return xc * lax.rsqrt(var + EPS) * g + b


def _inproj_ab_kernel(x_ref, wh_ref, wr_ref, ws_ref, o_ref, os_ref, xb_ref, *, n_head):
    j = pl.program_id(1)

    @pl.when(j == 0)
    def _():
        xb = x_ref[...].astype(BF16)
        xb_ref[...] = xb
        os_ref[...] = _dot(xb, ws_ref[...].astype(BF16))

    @pl.when(j < n_head)
    def _():
        o_ref[...] = _dot_nt(xb_ref[...], wh_ref[...].astype(BF16))

    @pl.when(j >= n_head)
    def _():
        o_ref[...] = _dot(xb_ref[...], wr_ref[...].astype(BF16))


def _inproj_ab(x, w_all, w_rest, layer, tm, tn):
    m, k = x.shape
    head_w = D_ATTN + 2 * KV_W + IDX_W
    n_head = head_w // tn
    rest_w = AB_MAIN_W - head_w
    vmem = 2 * tm * k * 4 + tm * k * 2 + 4 * k * tn * 4 + k * tn * 2 + 2 * tm * tn * 4 + 2 * k * AB_SMALL_W * 4 + 2 * tm * AB_SMALL_W * 4 + 4 * 2**20
    return pl.pallas_call(
        functools.partial(_inproj_ab_kernel, n_head=n_head),
        grid=(m // tm, AB_MAIN_W // tn),
        in_specs=[pl.BlockSpec((tm, k), lambda i, j: (i, 0)),
                  pl.BlockSpec((None, tn, k), lambda i, j: (layer, jnp.minimum(j, n_head - 1), 0)),
                  pl.BlockSpec((k, tn), lambda i, j: (0, jnp.maximum(j - n_head, 0))),
                  pl.BlockSpec((k, AB_SMALL_W), lambda i, j: (0, rest_w // AB_SMALL_W))],
        out_specs=[pl.BlockSpec((tm, tn), lambda i, j: (i, j)),
                   pl.BlockSpec((tm, AB_SMALL_W), lambda i, j: (i, 0))],
        out_shape=[jax.ShapeDtypeStruct((m, AB_MAIN_W), F32), jax.ShapeDtypeStruct((m, AB_SMALL_W), F32)],
        scratch_shapes=[pltpu.VMEM((tm, k), BF16)],
        compiler_params=_cparams(("parallel", "arbitrary"), vmem),
        name="inproj_ab",
    )(x, jnp.swapaxes(w_all, 1, 2), w_rest, w_rest)


def _inproj_kernel(x_ref, w_ref, o_ref, xb_ref):
    @pl.when(pl.program_id(1) == 0)
    def _():
        xb_ref[...] = x_ref[...].astype(BF16)

    o_ref[...] = _dot(xb_ref[...], w_ref[...].astype(BF16))


def _inproj(x, w_all, layer, tm, tn):
    m, k = x.shape
    n = w_all.shape[2]
    vmem = 2 * tm * k * 4 + tm * k * 2 + 2 * k * tn * 4 + k * tn * 2 + 2 * tm * tn * 4 + 4 * 2**20
    return pl.pallas_call(
        _inproj_kernel,
        grid=(m // tm, n // tn),
        in_specs=[pl.BlockSpec((tm, k), lambda i, j: (i, 0)),
                  pl.BlockSpec((None, k, tn), lambda i, j: (layer, 0, j))],
        out_specs=pl.BlockSpec((tm, tn), lambda i, j: (i, j)),
        out_shape=jax.ShapeDtypeStruct((m, n), F32),
        scratch_shapes=[pltpu.VMEM((tm, k), BF16)],
        compiler_params=_cparams(("parallel", "arbitrary"), vmem),
        name="inproj_cd",
    )(x, w_all)


def _outproj_ln_kernel(a_ref, b_ref, w_ref, x_ref, g_ref, beta_ref, o_ref, wb_ref, *, slab):
    k = pl.program_id(1)
    tm = x_ref.shape[0]
    wb_ref[...] = w_ref[...].astype(BF16)

    @pl.when(k == 0)
    def _():
        def first(rows):
            o_ref[rows, :] = _dot(a_ref[rows, :], wb_ref[...])

        _row_slabs(tm, slab, first, unroll=True)

    @pl.when(k == 1)
    def _():
        def second(rows):
            y = ALPHA * x_ref[rows, :] + o_ref[rows, :] + _dot(b_ref[rows, :], wb_ref[...])
            o_ref[rows, :] = _layer_norm(y, g_ref[...], beta_ref[...])

        _row_slabs(tm, slab, second, unroll=True)


def _outproj_ln(a, b, w_all, layer, x, g_all, beta_all, ln_idx, tm):
    m, d = x.shape
    kh = a.shape[1]
    slab = min(tm, 128)
    vmem = 4 * tm * kh * 2 + 2 * kh * d * 4 + kh * d * 2 + 4 * tm * d * 4 + 4 * slab * d * 4 + 4 * 2**20
    return pl.pallas_call(
        functools.partial(_outproj_ln_kernel, slab=slab),
        grid=(m // tm, 2),
        in_specs=[pl.BlockSpec((tm, kh), lambda i, k: (i, 0)),
                  pl.BlockSpec((tm, kh), lambda i, k: (i, 0)),
                  pl.BlockSpec((None, kh, d), lambda i, k: (layer, k, 0)),
                  pl.BlockSpec((tm, d), lambda i, k: (i, 0)),
                  pl.BlockSpec((None, 1, d), lambda i, k: (ln_idx, 0, 0)),
                  pl.BlockSpec((None, 1, d), lambda i, k: (ln_idx, 0, 0))],
        out_specs=pl.BlockSpec((tm, d), lambda i, k: (i, 0)),
        out_shape=jax.ShapeDtypeStruct((m, d), F32),
        scratch_shapes=[pltpu.VMEM((kh, d), BF16)],
        compiler_params=_cparams(("parallel", "arbitrary"), vmem),
        name="outproj_ln",
    )(a, b, w_all, x, g_all, beta_all)


def _row_slabs(n_rows, slab, body, unroll=False):
    if unroll:
        for r in range(n_rows // slab):
            body(pl.ds(r * slab, slab))
        return

    def step(r, carry):
        body(pl.ds(pl.multiple_of(r * slab, slab), slab))
        return carry

    lax.fori_loop(0, n_rows // slab, step, 0)


def _ffn_ln_kernel(x_ref, w1_ref, w2_ref, g_ref, beta_ref, o_ref, *rest, slab, mm_slab, emit_bf16):
    f = pl.program_id(1)
    tm = x_ref.shape[0]
    if emit_bf16:
        w1b_ref, w2b_ref, xb_ref = rest
        w1b_ref[...] = w1_ref[...].astype(BF16)
        w2b_ref[...] = w2_ref[...].astype(BF16)
    else:
        (xb_ref,) = rest
        w1b_ref, w2b_ref = w1_ref, w2_ref

    @pl.when(f == 0)
    def _():
        def init(rows):
            xb_ref[rows, :] = x_ref[rows, :].astype(BF16)
            o_ref[rows, :] = jnp.zeros((slab, o_ref.shape[1]), F32)

        _row_slabs(tm, slab, init)

    def accumulate(rows):
        h = jnp.maximum(_dot(xb_ref[rows, :], w1b_ref[...]), 0.0)
        o_ref[rows, :] += _dot((h * h).astype(BF16), w2b_ref[...])

    _row_slabs(tm, mm_slab, accumulate, unroll=True)

    @pl.when(f == pl.num_programs(1) - 1)
    def _():
        def finish(rows):
            o_ref[rows, :] = _layer_norm(ALPHA * x_ref[rows, :] + o_ref[rows, :], g_ref[...], beta_ref[...])

        _row_slabs(tm, slab, finish)


def _ffn_ln(x, w1, w2, g_all, beta_all, layer, tm, tf, emit_bf16):
    m, d = x.shape
    ff = w1.shape[-1]
    slab = min(tm, 256)
    mm_slab = min(tm, 128)
    w_bytes = 4 if emit_bf16 else 2
    vmem = (2 * tm * d * 4 + tm * d * 2 + 4 * d * tf * w_bytes + 2 * tm * d * 4
            + mm_slab * tf * 8 + 2 * mm_slab * d * 4 + 4 * slab * d * 4 + 4 * 2**20)
    x_spec = pl.BlockSpec((tm, d), lambda i, f: (i, 0))
    ln_spec = pl.BlockSpec((None, 1, d), lambda i, f: (layer, 0, 0))
    o_spec = pl.BlockSpec((tm, d), lambda i, f: (i, 0))
    o_shape = jax.ShapeDtypeStruct((m, d), F32)
    w1b_spec = pl.BlockSpec((d, tf), lambda i, f: (0, f))
    w2b_spec = pl.BlockSpec((tf, d), lambda i, f: (f, 0))
    if emit_bf16:
        assert m == tm, "the bf16 weight copies are written once, by a single row tile"
        vmem += 4 * d * tf * 2
        in_specs = [x_spec, pl.BlockSpec((None, d, tf), lambda i, f: (layer, 0, f)),
                    pl.BlockSpec((None, tf, d), lambda i, f: (layer, f, 0)), ln_spec, ln_spec]
        out_specs = [o_spec, w1b_spec, w2b_spec]
        out_shape = [o_shape, jax.ShapeDtypeStruct((d, ff), BF16), jax.ShapeDtypeStruct((ff, d), BF16)]
    else:
        in_specs = [x_spec, w1b_spec, w2b_spec, ln_spec, ln_spec]
        out_specs = o_spec
        out_shape = o_shape
    return pl.pallas_call(
        functools.partial(_ffn_ln_kernel, slab=slab, mm_slab=mm_slab, emit_bf16=emit_bf16),
        grid=(m // tm, ff // tf),
        in_specs=in_specs,
        out_specs=out_specs,
        out_shape=out_shape,
        scratch_shapes=[pltpu.VMEM((tm, d), BF16)],
        compiler_params=_cparams(("parallel", "arbitrary"), vmem),
        name="ffn_ln",
    )(x, w1, w2, g_all, beta_all)


def _sortable_key(score):
    score = jnp.where(score == 0.0, 0.0, score)
    bits = lax.bitcast_convert_type(score, I32)
    return jnp.where(bits < 0, bits ^ jnp.int32(0x7FFFFFFF), bits)


def _fold_lanes(m):
    acc = m[:, 0:V7X_LANES]
    for c in range(1, m.shape[1] // V7X_LANES):
        acc = acc + m[:, c * V7X_LANES:(c + 1) * V7X_LANES]
    return acc


def _fold_sublanes(m):
    return jnp.sum(m.reshape(m.shape[0] // V7X_SUBLANES, V7X_SUBLANES, m.shape[1]), axis=0)


def _select_topk(load_group, n_groups, n_queries, topk, index_bits, key_axis):
    if key_axis == 1:
        state, acc_shape, fold = (n_queries, 1), (n_queries, V7X_LANES), _fold_lanes
        group_shape = (n_queries, KEY_GROUP)
    else:
        state, acc_shape, fold = (1, n_queries), (V7X_SUBLANES, n_queries), _fold_sublanes
        group_shape = (KEY_GROUP, n_queries)

    def count(pred):
        def body(g, acc):
            return acc + fold(pred(load_group(g), g).astype(I32))

        acc = lax.fori_loop(0, n_groups, body, jnp.zeros(acc_shape, I32))
        return jnp.sum(acc, axis=key_axis, keepdims=True)

    def thr_pass(p, thr):
        cand = thr + lax.shift_left(jnp.int32(1), jnp.int32(31) - p)
        c = count(lambda k, g: k >= cand)
        return jnp.where(c >= topk, cand, thr)

    thr = lax.fori_loop(0, 32, thr_pass, jnp.full(state, INT_MIN, I32))
    c_ge = count(lambda k, g: k >= thr)
    c_gt = count(lambda k, g: k > thr)
    need = topk - c_gt
    has_tie = jnp.max(jnp.where((c_ge > topk) & (thr > INT_MIN), 1, 0)) > 0
    all_idx = jnp.full(state, (1 << index_bits) - 1, I32)

    def tie_path():
        pos = lax.broadcasted_iota(I32, group_shape, key_axis)

        def idx_pass(p, last):
            cand = last + lax.shift_left(jnp.int32(1), jnp.int32(index_bits - 1) - p)
            c = count(lambda k, g: (k == thr) & (pos + g * KEY_GROUP < cand))
            return jnp.where(c <= need - 1, cand, last)

        return lax.fori_loop(0, index_bits, idx_pass, jnp.zeros(state, I32))

    last = lax.cond(has_tie, tie_path, lambda: all_idx)
    return thr, last


def _selected_bias(keys, idx, thr, last):
    sel = ((keys > thr) | ((keys == thr) & (idx <= last))) & (keys > INT_MIN)
    return jnp.where(sel, 0.0, NEG_BIG).astype(F32)


def _stack_heads(qb, group, rows_per_head):
    rep = N_HEADS // N_KV_HEADS
    return jnp.concatenate(
        [qb[:, (group * rep + r) * HEAD_DIM:(group * rep + r + 1) * HEAD_DIM] for r in range(rep)], axis=0)


def _softmax_step(s, v_bf16, m_ref, l_ref, acc_ref, g):
    m_old = m_ref[g]
    m_new = jnp.maximum(m_old, jnp.max(s, axis=1, keepdims=True))
    p = jnp.exp(s - m_new)
    alpha = jnp.exp(m_old - m_new)
    l_ref[g] = alpha * l_ref[g] + jnp.sum(p, axis=1, keepdims=True)
    acc_ref[g] = alpha * acc_ref[g] + _dot(p.astype(BF16), v_bf16)
    m_ref[g] = m_new


def _dsa_prompt_kernel(q_ref, qi_ref, sq_ref, kv_ref, sk_ref, o_ref,
                       kb_ref, vt_ref, kib_ref, keys_ref, m_ref, l_ref, acc_ref, *, topk, index_bits):
    i = pl.program_id(1)
    tq = Q_BLOCK
    rep = N_HEADS // N_KV_HEADS
    seq = kv_ref.shape[0]

    @pl.when(i == 0)
    def _():
        kb_ref[...] = kv_ref[:, 0:KV_W].astype(BF16)
        kib_ref[...] = sk_ref[:, SMALL_KI:SMALL_KI + IDX_DIM].astype(BF16)

        def transpose_v(c, carry):
            k0 = pl.multiple_of(c * KEY_GROUP, KEY_GROUP)
            vt_ref[:, pl.ds(k0, KEY_GROUP)] = kv_ref[pl.ds(k0, KEY_GROUP), KV_W:2 * KV_W].T.astype(BF16)
            return carry

        lax.fori_loop(0, seq // KEY_GROUP, transpose_v, 0)

    n_groups = lax.shift_right_logical(i * tq + tq + KEY_GROUP - 1, int(math.log2(KEY_GROUP)))
    q_pos = i * tq + lax.broadcasted_iota(I32, (KEY_GROUP, tq), 1)
    key_row = lax.broadcasted_iota(I32, (KEY_GROUP, tq), 0)

    qi_t = qi_ref[...].T.astype(BF16)
    qi_all = jnp.concatenate([qi_t[h * IDX_DIM:(h + 1) * IDX_DIM, :] for h in range(N_IDX_HEADS)], axis=1)
    wi_t = sq_ref[...].T[SMALL_WI:SMALL_WI + N_IDX_HEADS, :]

    def score_body(g, carry):
        k0 = pl.multiple_of(g * KEY_GROUP, KEY_GROUP)
        dots = jnp.maximum(_dot(kib_ref[pl.ds(k0, KEY_GROUP), :], qi_all), 0.0)
        sc = jnp.zeros((KEY_GROUP, tq), F32)
        for h in range(N_IDX_HEADS):
            sc = sc + wi_t[h:h + 1, :] * dots[:, h * tq:(h + 1) * tq]
        keys = _sortable_key(sc * IDX_SCALE)
        keys_ref[pl.ds(k0, KEY_GROUP), :] = jnp.where(key_row + k0 <= q_pos, keys, INT_MIN)
        return carry

    lax.fori_loop(0, n_groups, score_body, 0)

    def load_group(g):
        return keys_ref[pl.ds(pl.multiple_of(g * KEY_GROUP, KEY_GROUP), KEY_GROUP), :]

    thr, last = _select_topk(load_group, n_groups, tq, topk, index_bits, key_axis=0)

    scale = HEAD_DIM ** -0.5
    q_t = (q_ref[...] * scale).T.astype(BF16)
    qg_t = [jnp.concatenate([q_t[(g * rep + r) * HEAD_DIM:(g * rep + r + 1) * HEAD_DIM, :] for r in range(rep)], axis=1)
            for g in range(N_KV_HEADS)]
    m_ref[...] = jnp.full(m_ref.shape, NEG_BIG, F32)
    l_ref[...] = jnp.zeros(l_ref.shape, F32)
    acc_ref[...] = jnp.zeros(acc_ref.shape, F32)

    def attn_body(g, carry):
        k0 = pl.multiple_of(g * KEY_GROUP, KEY_GROUP)
        bias = _selected_bias(keys_ref[pl.ds(k0, KEY_GROUP), :], key_row + k0, thr, last)
        bias = jnp.concatenate([bias] * rep, axis=1)
        for kvh in range(N_KV_HEADS):
            kc = kb_ref[pl.ds(k0, KEY_GROUP), kvh * HEAD_DIM:(kvh + 1) * HEAD_DIM]
            vc_t = vt_ref[kvh * HEAD_DIM:(kvh + 1) * HEAD_DIM, pl.ds(k0, KEY_GROUP)]
            s = _dot(kc, qg_t[kvh]) + bias
            m_old = m_ref[kvh]
            m_new = jnp.maximum(m_old, jnp.max(s, axis=0, keepdims=True))
            p = jnp.exp(s - m_new)
            alpha = jnp.exp(m_old - m_new)
            l_ref[kvh] = alpha * l_ref[kvh] + jnp.sum(p, axis=0, keepdims=True)
            acc_ref[kvh] = alpha * acc_ref[kvh] + _dot(vc_t, p.astype(BF16))
            m_ref[kvh] = m_new
        return carry

    lax.fori_loop(0, n_groups, attn_body, 0)

    for kvh in range(N_KV_HEADS):
        out_t = acc_ref[kvh] / l_ref[kvh]
        for r in range(rep):
            h = kvh * rep + r
            o_ref[:, h * HEAD_DIM:(h + 1) * HEAD_DIM] = out_t[:, r * tq:(r + 1) * tq].T.astype(o_ref.dtype)


def _dsa_prompt(h_main, h_small, batch, seq):
    tq = Q_BLOCK
    nq = seq // tq
    topk = min(TOPK_MAX, seq // 4)
    cols = N_HEADS // N_KV_HEADS * tq
    kv_blk = 2 * KV_W
    kernel = functools.partial(_dsa_prompt_kernel, topk=topk, index_bits=int(math.log2(seq)))
    vmem = (2 * seq * kv_blk * 4 + seq * kv_blk * 2 + 2 * seq * AB_SMALL_W * 4 + seq * V7X_LANES * 2 + tq * seq * 4
            + 2 * tq * (D_ATTN + IDX_W + AB_SMALL_W) * 4 + 2 * tq * D_ATTN * 2
            + N_KV_HEADS * cols * (2 * V7X_SUBLANES + HEAD_DIM) * 4 + 8 * cols * KEY_GROUP * 4 + 4 * 2**20)
    return pl.pallas_call(
        kernel,
        grid=(batch, nq),
        in_specs=[pl.BlockSpec((tq, D_ATTN), lambda b, i: (b * nq + i, 0)),
                  pl.BlockSpec((tq, IDX_W), lambda b, i: (b * nq + i, (D_ATTN + 2 * KV_W) // IDX_W)),
                  pl.BlockSpec((tq, AB_SMALL_W), lambda b, i: (b * nq + i, 0)),
                  pl.BlockSpec((seq, kv_blk), lambda b, i: (b, D_ATTN // kv_blk)),
                  pl.BlockSpec((seq, AB_SMALL_W), lambda b, i: (b, 0))],
        out_specs=pl.BlockSpec((tq, D_ATTN), lambda b, i: (b * nq + i, 0)),
        out_shape=jax.ShapeDtypeStruct((batch * seq, D_ATTN), BF16),
        scratch_shapes=[pltpu.VMEM((seq, KV_W), BF16),
                        pltpu.VMEM((KV_W, seq), BF16),
                        pltpu.VMEM((seq, IDX_DIM), BF16),
                        pltpu.VMEM((seq, tq), I32),
                        pltpu.VMEM((N_KV_HEADS, 1, cols), F32),
                        pltpu.VMEM((N_KV_HEADS, 1, cols), F32),
                        pltpu.VMEM((N_KV_HEADS, HEAD_DIM, cols), F32)],
        compiler_params=_cparams(("parallel", "arbitrary"), vmem),
        name="dsa_prompt",
    )(h_main, h_main, h_small, h_main, h_small)


def _dsa_sample_scores_kernel(pt_ref, qi_ref, sq_ref, *refs, pages_per_step):
    page_refs = refs[:pages_per_step]
    o_ref = refs[pages_per_step]
    n = qi_ref.shape[0]
    qib = qi_ref[...].astype(BF16)
    qs = jnp.concatenate([qib[:, h * IDX_DIM:(h + 1) * IDX_DIM] for h in range(N_IDX_HEADS)], axis=0)
    wi = sq_ref[:, SMALL_WI:SMALL_WI + N_IDX_HEADS]
    for t in range(pages_per_step):
        dots = jnp.maximum(_dot(qs, page_refs[t][...].astype(BF16)), 0.0)
        sc = jnp.zeros((n, PAGE_SIZE), F32)
        for h in range(N_IDX_HEADS):
            sc = sc + wi[:, h:h + 1] * dots[h * n:(h + 1) * n, :]
        o_ref[:, t * PAGE_SIZE:(t + 1) * PAGE_SIZE] = sc * IDX_SCALE


def _dsa_sample_scores(page_table, h_main, h_small, cache_ki, layer, n, pages_per_step):
    batch, n_pages = page_table.shape
    steps = n_pages // pages_per_step
    kernel = functools.partial(_dsa_sample_scores_kernel, pages_per_step=pages_per_step)

    def page_spec(t):
        return pl.BlockSpec((None, None, IDX_DIM, PAGE_SIZE),
                            lambda b, s, pt: (layer, pt[b, s * pages_per_step + t], 0, 0))

    grid_spec = pltpu.PrefetchScalarGridSpec(
        num_scalar_prefetch=1,
        grid=(batch, steps),
        in_specs=[pl.BlockSpec((n, IDX_W), lambda b, s, pt: (b, (D_ATTN + 2 * KV_W) // IDX_W)),
                  pl.BlockSpec((n, AB_SMALL_W), lambda b, s, pt: (b, 0))]
                 + [page_spec(t) for t in range(pages_per_step)],
        out_specs=pl.BlockSpec((None, n, pages_per_step * PAGE_SIZE), lambda b, s, pt: (b, 0, s)),
    )
    vmem = 4 * pages_per_step * PAGE_SIZE * V7X_LANES * 4 + 4 * n * pages_per_step * PAGE_SIZE * 4 + 8 * 2**20
    return pl.pallas_call(
        kernel,
        grid_spec=grid_spec,
        out_shape=jax.ShapeDtypeStruct((batch, n, n_pages * PAGE_SIZE), F32),
        compiler_params=_cparams(("parallel", "arbitrary"), vmem),
        name="dsa_sample_scores",
    )(page_table, h_main, h_small, *([cache_ki] * pages_per_step))


def _dsa_sample_attend_kernel(pt_ref, sc_ref, q_ref, qi_ref, sq_ref, kvn_ref, *refs,
                              pages_per_step, topk, index_bits, past):
    k_refs = refs[:pages_per_step]
    v_refs = refs[pages_per_step:2 * pages_per_step]
    o_ref = refs[2 * pages_per_step]
    keys_ref, thr_ref, last_ref, m_ref, l_ref, acc_ref, pad_ref = refs[2 * pages_per_step + 1:]
    s_idx = pl.program_id(1)
    n = q_ref.shape[0]
    rep = N_HEADS // N_KV_HEADS
    n_groups = keys_ref.shape[1] // KEY_GROUP
    lane_p = lax.broadcasted_iota(I32, (n, PAGE_SIZE), 1)
    scale = HEAD_DIM ** -0.5

    @pl.when(s_idx == 0)
    def _():
        keys_ref[...] = jnp.full(keys_ref.shape, INT_MIN, I32)
        keys_ref[:, 0:past] = _sortable_key(sc_ref[...])
        qib = qi_ref[...].astype(BF16)
        wi = sq_ref[:, SMALL_WI:SMALL_WI + N_IDX_HEADS]
        pad_ref[...] = jnp.zeros(pad_ref.shape, F32)
        pad_ref[0:n, 0:IDX_DIM] = sq_ref[:, SMALL_KI:SMALL_KI + IDX_DIM]
        ki_new = pad_ref[:, 0:IDX_DIM].astype(BF16)
        sc = jnp.zeros((n, PAGE_SIZE), F32)
        for h in range(N_IDX_HEADS):
            dots = _dot_nt(qib[:, h * IDX_DIM:(h + 1) * IDX_DIM], ki_new)
            sc = sc + wi[:, h:h + 1] * jnp.maximum(dots, 0.0)
        q_row = lax.broadcasted_iota(I32, (n, PAGE_SIZE), 0)
        keys_ref[:, past:past + PAGE_SIZE] = jnp.where(lane_p <= q_row, _sortable_key(sc * IDX_SCALE), INT_MIN)

        def load_group(g):
            return keys_ref[:, pl.ds(pl.multiple_of(g * KEY_GROUP, KEY_GROUP), KEY_GROUP)]

        thr, last = _select_topk(load_group, n_groups, n, topk, index_bits, key_axis=1)
        thr_ref[...] = thr
        last_ref[...] = last
        m_ref[...] = jnp.full(m_ref.shape, NEG_BIG, F32)
        l_ref[...] = jnp.zeros(l_ref.shape, F32)
        acc_ref[...] = jnp.zeros(acc_ref.shape, F32)

    thr = thr_ref[...]
    last = last_ref[...]
    qb = q_ref[...].astype(BF16)
    qg = [_stack_heads(qb, g, n) for g in range(N_KV_HEADS)]

    def bias_at(k0):
        keys = keys_ref[:, pl.ds(k0, PAGE_SIZE)]
        b = _selected_bias(keys, lane_p + k0, thr, last)
        return jnp.concatenate([b] * rep, axis=0)

    base = s_idx * (pages_per_step * PAGE_SIZE)
    bias = jnp.concatenate([bias_at(pl.multiple_of(base + t * PAGE_SIZE, PAGE_SIZE)) for t in range(pages_per_step)], axis=1)
    for kvh in range(N_KV_HEADS):
        head_rows = pl.ds(kvh, PAGE_SIZE, stride=N_KV_HEADS)
        kc = jnp.concatenate([k_refs[t][head_rows, :].astype(BF16) for t in range(pages_per_step)], axis=0)
        vc = jnp.concatenate([v_refs[t][head_rows, :].astype(BF16) for t in range(pages_per_step)], axis=0)
        s = _dot_nt(qg[kvh], kc) * scale + bias
        _softmax_step(s, vc, m_ref, l_ref, acc_ref, kvh)

    @pl.when(s_idx == pl.num_programs(1) - 1)
    def _():
        bias_n = bias_at(past)
        for kvh in range(N_KV_HEADS):
            pad_ref[0:n, :] = kvn_ref[:, kvh * HEAD_DIM:(kvh + 1) * HEAD_DIM]
            kc = pad_ref[...].astype(BF16)
            pad_ref[0:n, :] = kvn_ref[:, KV_W + kvh * HEAD_DIM:KV_W + (kvh + 1) * HEAD_DIM]
            vc = pad_ref[...].astype(BF16)
            s = _dot_nt(qg[kvh], kc) * scale + bias_n
            _softmax_step(s, vc, m_ref, l_ref, acc_ref, kvh)
        for kvh in range(N_KV_HEADS):
            out = acc_ref[kvh] / l_ref[kvh]
            for r in range(rep):
                h = kvh * rep + r
                o_ref[:, h * HEAD_DIM:(h + 1) * HEAD_DIM] = out[r * n:(r + 1) * n, :].astype(o_ref.dtype)


def _dsa_sample_attend(page_table, scores, h_main, h_small, cache_k, cache_v, layer, n, pages_per_step):
    batch, n_pages = page_table.shape
    past = n_pages * PAGE_SIZE
    steps = n_pages // pages_per_step
    topk = min(TOPK_MAX, (past + n) // 4)
    total = past + PAGE_SIZE
    padded = -(-total // KEY_GROUP) * KEY_GROUP
    index_bits = int(math.ceil(math.log2(padded)))
    rows = N_HEADS // N_KV_HEADS * n
    kernel = functools.partial(_dsa_sample_attend_kernel, pages_per_step=pages_per_step, topk=topk,
                               index_bits=index_bits, past=past)

    def page_spec(t):
        return pl.BlockSpec((None, None, PAGE_SIZE * N_KV_HEADS, HEAD_DIM),
                            lambda b, s, pt: (layer, pt[b, s * pages_per_step + t], 0, 0))

    kv_blk = 2 * KV_W
    grid_spec = pltpu.PrefetchScalarGridSpec(
        num_scalar_prefetch=1,
        grid=(batch, steps),
        in_specs=[pl.BlockSpec((None, n, past), lambda b, s, pt: (b, 0, 0)),
                  pl.BlockSpec((n, D_ATTN), lambda b, s, pt: (b, 0)),
                  pl.BlockSpec((n, IDX_W), lambda b, s, pt: (b, (D_ATTN + 2 * KV_W) // IDX_W)),
                  pl.BlockSpec((n, AB_SMALL_W), lambda b, s, pt: (b, 0)),
                  pl.BlockSpec((n, kv_blk), lambda b, s, pt: (b, D_ATTN // kv_blk))]
                 + [page_spec(t) for t in range(pages_per_step)] * 2,
        out_specs=pl.BlockSpec((n, D_ATTN), lambda b, s, pt: (b, 0)),
        scratch_shapes=[pltpu.VMEM((n, padded), I32),
                        pltpu.VMEM((n, 1), I32),
                        pltpu.VMEM((n, 1), I32),
                        pltpu.VMEM((N_KV_HEADS, rows, 1), F32),
                        pltpu.VMEM((N_KV_HEADS, rows, 1), F32),
                        pltpu.VMEM((N_KV_HEADS, rows, HEAD_DIM), F32),
                        pltpu.VMEM((PAGE_SIZE, HEAD_DIM), F32)],
    )
    vmem = (4 * 2 * pages_per_step * PAGE_SIZE * KV_W * 4 + 4 * n * padded * 4
            + 8 * rows * pages_per_step * PAGE_SIZE * 4 + 8 * 2**20)
    return pl.pallas_call(
        kernel,
        grid_spec=grid_spec,
        out_shape=jax.ShapeDtypeStruct((batch * n, D_ATTN), BF16),
        compiler_params=_cparams(("parallel", "arbitrary"), vmem),
        name="dsa_sample_attend",
    )(page_table, scores, h_main, h_main, h_small, h_main, *([cache_k] * pages_per_step), *([cache_v] * pages_per_step))


def _ssd_kernel(z_ref, xbc_ref, dtc_ref, dtr_ref, pre_ref, h0_ref, cw_ref, cb_ref, dtb_ref, alog_ref, dsk_ref,
                dtb_c_ref, alog_c_ref, g_ref, y_ref, hl_ref, state_ref, stage_ref, dtpad_ref, zpad_ref, *, t_in):
    c = pl.program_id(1)
    tt = SSD_CHUNK
    hpg = SSD_HEADS // SSD_GROUPS
    gw = hpg * SSD_HEAD_DIM

    @pl.when(c == 0)
    def _():
        state_ref[...] = h0_ref[...]
        stage_ref[0:V7X_SUBLANES, :] = pre_ref[...]
        if t_in < tt:
            stage_ref[V7X_SUBLANES:, :] = jnp.zeros((tt, CONV_DIM), F32)
            dtpad_ref[...] = jnp.zeros(dtpad_ref.shape, F32)
            zpad_ref[...] = jnp.zeros(zpad_ref.shape, F32)

    stage_ref[V7X_SUBLANES:V7X_SUBLANES + t_in, :] = xbc_ref[...]
    conv = cb_ref[...] + jnp.zeros((tt, CONV_DIM), F32)
    for j in range(CONV_W):
        off = V7X_SUBLANES - (CONV_W - 1) + j
        conv = conv + cw_ref[j:j + 1, :] * stage_ref[off:off + tt, :]
    if t_in == tt:
        stage_ref[0:V7X_SUBLANES, :] = xbc_ref[tt - V7X_SUBLANES:tt, :]
    xbc = conv * jax.nn.sigmoid(conv)
    xs = xbc[:, 0:D_SSD]

    row_valid = lax.broadcasted_iota(I32, (tt, 1), 0) < t_in
    lane_valid = lax.broadcasted_iota(I32, (1, tt), 1) < t_in
    if t_in < tt:
        dtpad_ref[0:t_in, :] = dtc_ref[...]
        dt_raw_c = dtpad_ref[...]
        dtpad_ref[0:SSD_HEADS, 0:t_in] = dtr_ref[...]
        dt_raw_r = dtpad_ref[0:SSD_HEADS, :]
        zpad_ref[0:t_in, :] = z_ref[...]
        z = zpad_ref[...]
    else:
        dt_raw_c = dtc_ref[...]
        dt_raw_r = dtr_ref[...]
        z = z_ref[...]

    hl = lax.broadcasted_iota(I32, (1, AB_SMALL_W), 1)
    head_lane = (hl >= SMALL_DT) & (hl < SMALL_DT + SSD_HEADS)
    dt_c = jnp.where(row_valid & head_lane, jax.nn.softplus(dt_raw_c + dtb_ref[...]), 0.0)
    dt_r = jnp.where(lane_valid, jax.nn.softplus(dt_raw_r + dtb_c_ref[...]), 0.0)
    a_c = -jnp.exp(alog_ref[...])
    a_r = -jnp.exp(alog_c_ref[...])
    ri = lax.broadcasted_iota(I32, (tt, tt), 0)
    ci = lax.broadcasted_iota(I32, (tt, tt), 1)
    tril = ri >= ci
    acum_c = _dot_exact(jnp.where(tril, 1.0, 0.0).astype(F32), dt_c * a_c)
    acum_r = _dot_exact(dt_r * a_r, jnp.where(ri <= ci, 1.0, 0.0).astype(F32))
    acum_last = acum_c[tt - 1:tt, :]

    eh = lax.broadcasted_iota(I32, (AB_SMALL_W, D_SSD), 0) - SMALL_DT
    ef = lax.broadcasted_iota(I32, (AB_SMALL_W, D_SSD), 1)
    expand = jnp.where(lax.shift_right_logical(ef, int(math.log2(SSD_HEAD_DIM))) == eh, 1.0, 0.0).astype(F32)
    decay_in = _dot_exact(jnp.exp(acum_c), expand)
    decay_out = _dot_exact(jnp.exp(acum_last - acum_c) * dt_c, expand)
    chunk_decay = _dot_exact(jnp.exp(acum_last) + jnp.zeros((V7X_SUBLANES, 1), F32), expand)[0:1, :]
    d_skip = _dot_exact(dsk_ref[...] + jnp.zeros((V7X_SUBLANES, 1), F32), expand)[0:1, :]

    xs_b = xs.astype(BF16)
    xw_b = (xs * decay_out).astype(BF16)
    lane128 = lax.broadcasted_iota(I32, (tt, 2 * SSD_HEAD_DIM), 1)
    y_parts = []
    new_state = []
    for g in range(SSD_GROUPS):
        bm = xbc[:, D_SSD + g * SSD_STATE:D_SSD + (g + 1) * SSD_STATE].astype(BF16)
        cm = xbc[:, D_SSD + (SSD_GROUPS + g) * SSD_STATE:D_SSD + (SSD_GROUPS + g + 1) * SSD_STATE].astype(BF16)
        cb = _dot_nt(cm, bm)
        st_g = state_ref[:, g * gw:(g + 1) * gw]
        y_off = _dot(cm, st_g.astype(BF16)) * decay_in[:, g * gw:(g + 1) * gw]
        new_state.append(st_g * chunk_decay[:, g * gw:(g + 1) * gw] + _dot_tn(bm, xw_b[:, g * gw:(g + 1) * gw]))
        for pair in range(hpg // 2):
            ms = []
            for hh in range(2):
                h = g * hpg + 2 * pair + hh
                seg = jnp.where(tril, acum_c[:, SMALL_DT + h:SMALL_DT + h + 1] - acum_r[h:h + 1, :], -jnp.inf)
                ms.append((cb * jnp.exp(seg) * dt_r[h:h + 1, :]).astype(BF16))
            col = (g * hpg + 2 * pair) * SSD_HEAD_DIM
            xp = xs_b[:, col:col + 2 * SSD_HEAD_DIM]
            zero = jnp.zeros_like(xp)
            rhs = jnp.concatenate([jnp.where(lane128 < SSD_HEAD_DIM, xp, zero),
                                   jnp.where(lane128 >= SSD_HEAD_DIM, xp, zero)], axis=0)
            y_parts.append(_dot(jnp.concatenate(ms, axis=1), rhs) + y_off[:, col - g * gw:col - g * gw + 2 * SSD_HEAD_DIM])
    for g in range(SSD_GROUPS):
        state_ref[:, g * gw:(g + 1) * gw] = new_state[g]

    y = jnp.concatenate(y_parts, axis=1) + d_skip * xs
    y = y * (z * jax.nn.sigmoid(z))
    y = y * lax.rsqrt(jnp.mean(y * y, axis=-1, keepdims=True) + EPS) * g_ref[...]
    y_ref[...] = y[0:t_in, :].astype(y_ref.dtype)

    @pl.when(c == pl.num_programs(1) - 1)
    def _():
        hl_ref[...] = state_ref[...]


def _ssd(h_main, h_small, dt_rows, prefix, h0, conv_w, conv_b, dt_bias, a_log, d_skip, ssd_g, batch, n_chunks, t_in):
    tt = SSD_CHUNK
    kernel = functools.partial(_ssd_kernel, t_in=t_in)
    z_blk = (D_ATTN + 2 * KV_W + IDX_W) // D_SSD
    xbc_blk = (D_ATTN + 2 * KV_W + IDX_W + D_SSD) // CONV_DIM
    row = lambda v: v.reshape(1, -1)
    col = lambda v: v.reshape(-1, 1)
    head_row = lambda v: jnp.pad(v.reshape(1, -1), ((0, 0), (SMALL_DT, AB_SMALL_W - SMALL_DT - SSD_HEADS)))
    full = lambda shape: pl.BlockSpec(shape, lambda b, c: (0,) * len(shape))
    vmem = 40 * tt * CONV_DIM * 4 + 8 * SSD_STATE * D_SSD * 4 + 8 * 2**20
    return pl.pallas_call(
        kernel,
        grid=(batch, n_chunks),
        in_specs=[pl.BlockSpec((t_in, D_SSD), lambda b, c: (b * n_chunks + c, z_blk)),
                  pl.BlockSpec((t_in, CONV_DIM), lambda b, c: (b * n_chunks + c, xbc_blk)),
                  pl.BlockSpec((t_in, AB_SMALL_W), lambda b, c: (b * n_chunks + c, 0)),
                  pl.BlockSpec((None, SSD_HEADS, t_in), lambda b, c: (b * n_chunks + c, 0, 0)),
                  pl.BlockSpec((None, V7X_SUBLANES, CONV_DIM), lambda b, c: (b, 0, 0)),
                  pl.BlockSpec((None, SSD_STATE, D_SSD), lambda b, c: (b, 0, 0)),
                  full((CONV_W, CONV_DIM)), full((1, CONV_DIM)), full((1, AB_SMALL_W)), full((1, AB_SMALL_W)),
                  full((1, AB_SMALL_W)), full((SSD_HEADS, 1)), full((SSD_HEADS, 1)), full((1, D_SSD))],
        out_specs=[pl.BlockSpec((t_in, D_SSD), lambda b, c: (b * n_chunks + c, 0)),
                   pl.BlockSpec((None, SSD_STATE, D_SSD), lambda b, c: (b, 0, 0))],
        out_shape=[jax.ShapeDtypeStruct((batch * n_chunks * t_in, D_SSD), BF16),
                   jax.ShapeDtypeStruct((batch, SSD_STATE, D_SSD), F32)],
        scratch_shapes=[pltpu.VMEM((SSD_STATE, D_SSD), F32),
                        pltpu.VMEM((V7X_SUBLANES + tt, CONV_DIM), F32),
                        pltpu.VMEM((tt, AB_SMALL_W), F32),
                        pltpu.VMEM((tt, D_SSD), F32)],
        compiler_params=_cparams(("parallel", "arbitrary"), vmem),
        name="ssd",
    )(h_main, h_main, h_small, dt_rows, prefix, h0, conv_w, row(conv_b), head_row(dt_bias), head_row(a_log), head_row(d_skip),
      col(dt_bias), col(a_log), row(ssd_g))


def _cd_kernel(p_ref, u_ref, v_ref, pre_ref, pw_ref, ps_ref, lg_ref, lb_ref, sw_ref, sb_ref,
               pooled_ref, gated_ref, vout_ref, stage_ref, upad_ref, vpad_ref, *, t_in, prefix_len):
    c = pl.program_id(1)
    tt = SGU_CHUNK
    hist = max(POOL_WINDOWS)

    @pl.when(c == 0)
    def _():
        stage_ref[0:hist, :] = pre_ref[...]
        if t_in < tt:
            stage_ref[hist:, :] = jnp.zeros((tt, POOL_W), F32)
            upad_ref[...] = jnp.zeros(upad_ref.shape, F32)
            vpad_ref[...] = jnp.zeros(vpad_ref.shape, F32)

    stage_ref[hist:hist + t_in, :] = p_ref[...]
    end = prefix_len + c * tt + lax.broadcasted_iota(I32, (tt, 1), 0) + 1
    pooled = []
    for j, w in enumerate(POOL_WINDOWS):
        cols = slice(j * POOL_GW, (j + 1) * POOL_GW)
        cur = stage_ref[hist:hist + tt, cols]
        tot = cur
        for k in range(1, w):
            tot = tot + stage_ref[hist - k:hist - k + tt, cols]
        cnt = jnp.minimum(end, w).astype(F32)
        d = (tot / cnt - cur).astype(BF16)
        pooled.append(_dot(d, pw_ref[j].astype(BF16)))
    if t_in == tt:
        stage_ref[0:hist, :] = p_ref[tt - hist:tt, :]
    pooled = jnp.concatenate(pooled, axis=1) * ps_ref[...]
    pooled_ref[...] = pooled[0:t_in, :].astype(pooled_ref.dtype)

    gelu = lambda x: x * (lax.erf(x / math.sqrt(2.0)) + 1.0) / 2.0
    u = gelu(u_ref[...])
    v = _layer_norm(gelu(v_ref[...]), lg_ref[...], lb_ref[...])
    vout_ref[...] = v
    if t_in < tt:
        upad_ref[0:t_in, :] = u
        vpad_ref[0:t_in, :] = v
        u = upad_ref[...]
        v = vpad_ref[...]
    ri = lax.broadcasted_iota(I32, (tt, tt), 0)
    ci = lax.broadcasted_iota(I32, (tt, tt), 1)
    vb = v.astype(BF16)
    gates = []
    for g in range(SGU_GROUPS):
        wm = jnp.where(ri >= ci, sw_ref[g], 0.0).astype(BF16)
        gates.append(_dot(wm, vb[:, g * SGU_GW:(g + 1) * SGU_GW]) + sb_ref[:, g:g + 1])
    gated = u * jnp.concatenate(gates, axis=1)
    gated_ref[...] = gated[0:t_in, :].astype(gated_ref.dtype)


def _cd_mixer(h, prefix, pool_w, pool_scale, ln_g, ln_b, sgu_w, sgu_b, batch, n_chunks, t_in, prefix_len):
    tt = SGU_CHUNK
    hist = max(POOL_WINDOWS)
    kernel = functools.partial(_cd_kernel, t_in=t_in, prefix_len=prefix_len)
    row = lambda v: v.reshape(1, -1)
    full = lambda shape: pl.BlockSpec(shape, lambda b, c: (0,) * len(shape))
    m = batch * n_chunks * t_in
    vmem = 40 * tt * POOL_W * 4 + 4 * POOL_GW * POOL_GW * len(POOL_WINDOWS) * 4 + 8 * 2**20
    return pl.pallas_call(
        kernel,
        grid=(batch, n_chunks),
        in_specs=[pl.BlockSpec((t_in, POOL_W), lambda b, c: (b * n_chunks + c, 0)),
                  pl.BlockSpec((t_in, SGU_W), lambda b, c: (b * n_chunks + c, 1)),
                  pl.BlockSpec((t_in, SGU_W), lambda b, c: (b * n_chunks + c, 2)),
                  pl.BlockSpec((None, hist, POOL_W), lambda b, c: (b, 0, 0)),
                  full((len(POOL_WINDOWS), POOL_GW, POOL_GW)), full((1, POOL_W)), full((1, SGU_W)), full((1, SGU_W)),
                  full((SGU_GROUPS, tt, tt)), full((tt, SGU_GROUPS))],
        out_specs=[pl.BlockSpec((t_in, POOL_W), lambda b, c: (b * n_chunks + c, 0)),
                   pl.BlockSpec((t_in, SGU_W), lambda b, c: (b * n_chunks + c, 0)),
                   pl.BlockSpec((t_in, SGU_W), lambda b, c: (b * n_chunks + c, 0))],
        out_shape=[jax.ShapeDtypeStruct((m, POOL_W), BF16),
                   jax.ShapeDtypeStruct((m, SGU_W), BF16),
                   jax.ShapeDtypeStruct((m, SGU_W), F32)],
        scratch_shapes=[pltpu.VMEM((hist + tt, POOL_W), F32),
                        pltpu.VMEM((tt, SGU_W), F32),
                        pltpu.VMEM((tt, SGU_W), F32)],
        compiler_params=_cparams(("parallel", "arbitrary"), vmem),
        name="cd_mixer",
    )(h, h, h, prefix, pool_w, row(pool_scale), row(ln_g), row(ln_b), sgu_w, sgu_b.T)


def _row_tile(m, cap):
    t = min(m, cap)
    while m % t:
        t //= 2
    return t


def _ab_layer(yp, ys, j, i, bp, sp, bs, ns, cache_k, cache_v, cache_idx_k, state_ssm, state_conv, page_table,
              w_in_ab, conv_w, conv_b, dt_bias, a_log, d_skip, ssd_norm_g, w_out_ab, ln_g, ln_b):
    o = [0]
    for s in AB_SIZES:
        o.append(o[-1] + s)
    w_rest = jnp.concatenate([w_in_ab[j, :, o[6]:o[8]], w_in_ab[j, :, o[4]:o[6]], w_in_ab[j, :, o[8]:o[9]],
                              jnp.zeros((D_MODEL, AB_SMALL_W - (o[6] - o[4]) - (o[9] - o[8])), w_in_ab.dtype)], axis=1)
    outs = []
    states = []
    for x, batch, n in ((yp, bp, sp), (ys, bs, ns)):
        m = batch * n
        hm, hs = _inproj_ab(x, w_in_ab, w_rest, j, _row_tile(m, 1024), 512)
        prompt = n % SSD_CHUNK == 0
        t_in = SSD_CHUNK if prompt else n
        n_chunks = n // t_in
        dt_rows = hs[:, SMALL_DT:SMALL_DT + SSD_HEADS].reshape(batch * n_chunks, t_in, SSD_HEADS).transpose(0, 2, 1)
        if prompt:
            attn = _dsa_prompt(hm, hs, batch, n)
            prefix = jnp.zeros((batch, V7X_SUBLANES, CONV_DIM), F32)
            h0 = jnp.zeros((batch, SSD_STATE, D_SSD), F32)
        else:
            pages = 16
            page_view = cache_k.shape[:2] + (PAGE_SIZE * N_KV_HEADS, HEAD_DIM)
            scores = _dsa_sample_scores(page_table, hm, hs, jnp.swapaxes(cache_idx_k, 2, 3), j, n, pages)
            attn = _dsa_sample_attend(page_table, scores, hm, hs, cache_k.reshape(page_view), cache_v.reshape(page_view),
                                      j, n, pages)
            prefix = jnp.concatenate([jnp.zeros((batch, V7X_SUBLANES - (CONV_W - 1), CONV_DIM), F32), state_conv[j]], axis=1)
            h0 = state_ssm[j].transpose(0, 3, 1, 2).reshape(batch, SSD_STATE, D_SSD)
        y_ssd, h_last = _ssd(hm, hs, dt_rows, prefix, h0, conv_w[j], conv_b[j], dt_bias[j], a_log[j], d_skip[j],
                             ssd_norm_g[j], batch, n_chunks, t_in)
        outs.append(_outproj_ln(attn, y_ssd, w_out_ab, j, x, ln_g, ln_b, i, _row_tile(m, 512)))
        k = hm[:, o[1]:o[2]].reshape(batch, n, N_KV_HEADS, HEAD_DIM)
        v = hm[:, o[2]:o[3]].reshape(batch, n, N_KV_HEADS, HEAD_DIM)
        ki = hs[:, SMALL_KI:SMALL_KI + IDX_DIM].reshape(batch, n, IDX_DIM)
        ssm = h_last.reshape(batch, SSD_STATE, SSD_HEADS, SSD_HEAD_DIM).transpose(0, 2, 3, 1)
        xbc_off = o[4] + D_SSD
        conv_new = hm.reshape(batch, n, AB_MAIN_W)[:, n - (CONV_W - 1):, xbc_off:xbc_off + CONV_DIM]
        states.append((k, v, ki, ssm, conv_new))
    return outs, states


def _cd_layer(yp, ys, j, i, bp, sp, bs, ns, state_pool, w_in_cd, pool_w, pool_scale, sgu_ln_g, sgu_ln_b, sgu_w, sgu_b,
              w_out_cd, ln_g, ln_b):
    hist = max(POOL_WINDOWS)
    outs = []
    states = []
    for x, batch, n in ((yp, bp, sp), (ys, bs, ns)):
        m = batch * n
        h = _inproj(x, w_in_cd, j, _row_tile(m, 1024), 512)
        prompt = n % SGU_CHUNK == 0
        t_in = SGU_CHUNK if prompt else n
        n_chunks = n // t_in
        p = h.reshape(batch, n, h.shape[1])[:, max(n - POOL_STATE, 0):, 0:POOL_W]
        if prompt:
            prefix = jnp.zeros((batch, hist, POOL_W), F32)
            prefix_len = 0
            pool_state = p
        else:
            prefix = jnp.concatenate([jnp.zeros((batch, hist - POOL_STATE, POOL_W), F32), state_pool[j]], axis=1)
            prefix_len = POOL_STATE
            pool_state = jnp.concatenate([state_pool[j], p], axis=1)[:, n:]
        pooled, gated, v = _cd_mixer(h, prefix, pool_w[j], pool_scale[j], sgu_ln_g[j], sgu_ln_b[j], sgu_w[j], sgu_b[j],
                                     batch, n_chunks, t_in, prefix_len)
        outs.append(_outproj_ln(pooled, gated, w_out_cd, j, x, ln_g, ln_b, i, _row_tile(m, 512)))
        states.append((pool_state, v.reshape(batch, n, SGU_W)))
    return outs, states


def kernel(x_prompt, x_sample, cache_k, cache_v, cache_idx_k, state_ssm, state_conv, state_pool, page_table, ln1_g, ln1_b, ln2_g, ln2_b, w_in_ab, conv_w, conv_b, dt_bias, a_log, d_skip, ssd_norm_g, w_out_ab, w_in_cd, pool_w, pool_scale, sgu_ln_g, sgu_ln_b, sgu_w, sgu_b, w_out_cd, w_ff1, w_ff2):
    bp, sp, d = x_prompt.shape
    bs, ns, _ = x_sample.shape
    depth = w_ff1.shape[0]
    yp = x_prompt.reshape(bp * sp, d)
    ys = x_sample.reshape(bs * ns, d)
    ab_p, ab_s, cd_p, cd_s = [], [], [], []
    ln1 = (ln1_g.reshape(depth, 1, d), ln1_b.reshape(depth, 1, d))
    ln2 = (ln2_g.reshape(depth, 1, d), ln2_b.reshape(depth, 1, d))
    for i in range(depth):
        j = i // 2
        if i % 2 == 0:
            (yp, ys), (st_p, st_s) = _ab_layer(yp, ys, j, i, bp, sp, bs, ns, cache_k, cache_v, cache_idx_k, state_ssm,
                                               state_conv, page_table, w_in_ab, conv_w, conv_b, dt_bias, a_log, d_skip,
                                               ssd_norm_g, w_out_ab, *ln1)
            ab_p.append(st_p)
            ab_s.append(st_s)
        else:
            (yp, ys), (st_p, st_s) = _cd_layer(yp, ys, j, i, bp, sp, bs, ns, state_pool, w_in_cd, pool_w, pool_scale,
                                               sgu_ln_g, sgu_ln_b, sgu_w, sgu_b, w_out_cd, *ln1)
            cd_p.append(st_p)
            cd_s.append(st_s)
        ys, w1b, w2b = _ffn_ln(ys, w_ff1, w_ff2, *ln2, i, bs * ns, 256, emit_bf16=True)
        yp = _ffn_ln(yp, w1b, w2b, *ln2, i, _row_tile(bp * sp, 1024), 512, emit_bf16=False)
    stack = lambda lst, idx: jnp.stack([t[idx] for t in lst])
    return (yp.reshape(bp, sp, d), ys.reshape(bs, ns, d),
            stack(ab_p, 0), stack(ab_p, 1), stack(ab_p, 2), stack(ab_p, 3), stack(ab_p, 4), stack(cd_p, 0),
            stack(ab_s, 0), stack(ab_s, 1), stack(ab_s, 2), stack(ab_s, 3), stack(ab_s, 4), stack(cd_s, 0), stack(cd_s, 1))
```

```python
import functools
import math

import jax
import jax.numpy as jnp
from jax import lax
from jax.experimental import pallas as pl
from jax.experimental.pallas import tpu as pltpu

D_MODEL = 2048
PAGE_SIZE = 128
N_HEADS = 8
N_KV_HEADS = 2
HEAD_DIM = 128
D_ATTN = N_HEADS * HEAD_DIM
KV_W = N_KV_HEADS * HEAD_DIM
N_IDX_HEADS = 8
IDX_DIM = 64
IDX_W = N_IDX_HEADS * IDX_DIM
IDX_SCALE = (N_IDX_HEADS * IDX_DIM) ** -0.5
TOPK_MAX = 256
Q_BLOCK = 128
SSD_HEADS = 16
SSD_HEAD_DIM = 64
D_SSD = SSD_HEADS * SSD_HEAD_DIM
SSD_GROUPS = 2
SSD_STATE = 128
CONV_W = 4
CONV_DIM = D_SSD + 2 * SSD_GROUPS * SSD_STATE
SSD_CHUNK = 128
POOL_WINDOWS = (2, 4, 8, 16)
POOL_W = D_MODEL // 2
POOL_GW = POOL_W // len(POOL_WINDOWS)
POOL_STATE = max(POOL_WINDOWS) - 1
SGU_W = D_MODEL // 2
SGU_GROUPS = 4
SGU_GW = SGU_W // SGU_GROUPS
SGU_CHUNK = 128
DEPTH = 4
ALPHA = (2 * DEPTH) ** 0.25
EPS = 1e-5
AB_SIZES = (D_ATTN, KV_W, KV_W, IDX_W, IDX_DIM, N_IDX_HEADS, D_SSD, CONV_DIM, SSD_HEADS)

AB_MAIN_W = D_ATTN + 2 * KV_W + IDX_W + D_SSD + CONV_DIM
AB_SMALL_W = 128
SMALL_KI = 0
SMALL_WI = IDX_DIM
SMALL_DT = IDX_DIM + N_IDX_HEADS

V7X_LANES = 128
V7X_SUBLANES = 8
V7X_VMEM_BYTES = 64 * 2**20

KEY_GROUP = 512
INT_MIN = -2**31
NEG_BIG = -1e30

F32 = jnp.float32
BF16 = jnp.bfloat16
I32 = jnp.int32


def _cparams(sem, vmem_bytes):
    return pltpu.CompilerParams(dimension_semantics=sem, vmem_limit_bytes=int(min(vmem_bytes, V7X_VMEM_BYTES - 4 * 2**20)))


def _dot(a, b):
    return jnp.dot(a, b, preferred_element_type=F32)


def _dot_nt(a, b):
    return lax.dot_general(a, b, (((1,), (1,)), ((), ())), preferred_element_type=F32)


def _dot_tn(a, b):
    return lax.dot_general(a, b, (((0,), (0,)), ((), ())), preferred_element_type=F32)


def _dot_exact(a, b):
    return jnp.dot(a, b, preferred_element_type=F32, precision=lax.Precision.HIGHEST)


def _layer_norm(x, g, b):
    mu = jnp.mean(x, axis=-1, keepdims=True)
    xc = x - mu
    var = jnp.mean(xc * xc, axis=-1, keepdims=True)
    return xc * lax.rsqrt(var + EPS) * g + b


def _inproj_ab_kernel(x_ref, wh_ref, wr_ref, ws_ref, o_ref, os_ref, xb_ref, *, n_head):
    j = pl.program_id(1)

    @pl.when(j == 0)
    def _():
        xb = x_ref[...].astype(BF16)
        xb_ref[...] = xb
        os_ref[...] = _dot(xb, ws_ref[...].astype(BF16))

    @pl.when(j < n_head)
    def _():
        o_ref[...] = _dot_nt(xb_ref[...], wh_ref[...].astype(BF16))

    @pl.when(j >= n_head)
    def _():
        o_ref[...] = _dot(xb_ref[...], wr_ref[...].astype(BF16))


def _inproj_ab(x, w_all, w_rest, layer, tm, tn):
    m, k = x.shape
    head_w = D_ATTN + 2 * KV_W + IDX_W
    n_head = head_w // tn
    rest_w = AB_MAIN_W - head_w
    vmem = 2 * tm * k * 4 + tm * k * 2 + 4 * k * tn * 4 + k * tn * 2 + 2 * tm * tn * 4 + 2 * k * AB_SMALL_W * 4 + 2 * tm * AB_SMALL_W * 4 + 4 * 2**20
    return pl.pallas_call(
        functools.partial(_inproj_ab_kernel, n_head=n_head),
        grid=(m // tm, AB_MAIN_W // tn),
        in_specs=[pl.BlockSpec((tm, k), lambda i, j: (i, 0)),
                  pl.BlockSpec((None, tn, k), lambda i, j: (layer, jnp.minimum(j, n_head - 1), 0)),
                  pl.BlockSpec((k, tn), lambda i, j: (0, jnp.maximum(j - n_head, 0))),
                  pl.BlockSpec((k, AB_SMALL_W), lambda i, j: (0, rest_w // AB_SMALL_W))],
        out_specs=[pl.BlockSpec((tm, tn), lambda i, j: (i, j)),
                   pl.BlockSpec((tm, AB_SMALL_W), lambda i, j: (i, 0))],
        out_shape=[jax.ShapeDtypeStruct((m, AB_MAIN_W), F32), jax.ShapeDtypeStruct((m, AB_SMALL_W), F32)],
        scratch_shapes=[pltpu.VMEM((tm, k), BF16)],
        compiler_params=_cparams(("parallel", "arbitrary"), vmem),
        name="inproj_ab",
    )(x, jnp.swapaxes(w_all, 1, 2), w_rest, w_rest)


def _inproj_kernel(x_ref, w_ref, o_ref, xb_ref):
    @pl.when(pl.program_id(1) == 0)
    def _():
        xb_ref[...] = x_ref[...].astype(BF16)

    o_ref[...] = _dot(xb_ref[...], w_ref[...].astype(BF16))


def _inproj(x, w_all, layer, tm, tn):
    m, k = x.shape
    n = w_all.shape[2]
    vmem = 2 * tm * k * 4 + tm * k * 2 + 2 * k * tn * 4 + k * tn * 2 + 2 * tm * tn * 4 + 4 * 2**20
    return pl.pallas_call(
        _inproj_kernel,
        grid=(m // tm, n // tn),
        in_specs=[pl.BlockSpec((tm, k), lambda i, j: (i, 0)),
                  pl.BlockSpec((None, k, tn), lambda i, j: (layer, 0, j))],
        out_specs=pl.BlockSpec((tm, tn), lambda i, j: (i, j)),
        out_shape=jax.ShapeDtypeStruct((m, n), F32),
        scratch_shapes=[pltpu.VMEM((tm, k), BF16)],
        compiler_params=_cparams(("parallel", "arbitrary"), vmem),
        name="inproj_cd",
    )(x, w_all)


def _outproj_ln_kernel(a_ref, b_ref, w_ref, x_ref, g_ref, beta_ref, o_ref, wb_ref, *, slab):
    k = pl.program_id(1)
    tm = x_ref.shape[0]
    wb_ref[...] = w_ref[...].astype(BF16)

    @pl.when(k == 0)
    def _():
        def first(rows):
            o_ref[rows, :] = _dot(a_ref[rows, :], wb_ref[...])

        _row_slabs(tm, slab, first, unroll=True)

    @pl.when(k == 1)
    def _():
        def second(rows):
            y = ALPHA * x_ref[rows, :] + o_ref[rows, :] + _dot(b_ref[rows, :], wb_ref[...])
            o_ref[rows, :] = _layer_norm(y, g_ref[...], beta_ref[...])

        _row_slabs(tm, slab, second, unroll=True)


def _outproj_ln(a, b, w_all, layer, x, g_all, beta_all, ln_idx, tm):
    m, d = x.shape
    kh = a.shape[1]
    slab = min(tm, 128)
    vmem = 4 * tm * kh * 2 + 2 * kh * d * 4 + kh * d * 2 + 4 * tm * d * 4 + 4 * slab * d * 4 + 4 * 2**20
    return pl.pallas_call(
        functools.partial(_outproj_ln_kernel, slab=slab),
        grid=(m // tm, 2),
        in_specs=[pl.BlockSpec((tm, kh), lambda i, k: (i, 0)),
                  pl.BlockSpec((tm, kh), lambda i, k: (i, 0)),
                  pl.BlockSpec((None, kh, d), lambda i, k: (layer, k, 0)),
                  pl.BlockSpec((tm, d), lambda i, k: (i, 0)),
                  pl.BlockSpec((None, 1, d), lambda i, k: (ln_idx, 0, 0)),
                  pl.BlockSpec((None, 1, d), lambda i, k: (ln_idx, 0, 0))],
        out_specs=pl.BlockSpec((tm, d), lambda i, k: (i, 0)),
        out_shape=jax.ShapeDtypeStruct((m, d), F32),
        scratch_shapes=[pltpu.VMEM((kh, d), BF16)],
        compiler_params=_cparams(("parallel", "arbitrary"), vmem),
        name="outproj_ln",
    )(a, b, w_all, x, g_all, beta_all)


def _row_slabs(n_rows, slab, body, unroll=False):
    if unroll:
        for r in range(n_rows // slab):
            body(pl.ds(r * slab, slab))
        return

    def step(r, carry):
        body(pl.ds(pl.multiple_of(r * slab, slab), slab))
        return carry

    lax.fori_loop(0, n_rows // slab, step, 0)


def _ffn_ln_kernel(x_ref, w1_ref, w2_ref, g_ref, beta_ref, o_ref, xb_ref, w1b_ref, w2b_ref, *, slab, mm_slab):
    f = pl.program_id(1)
    tm = x_ref.shape[0]

    @pl.when(f == 0)
    def _():
        def init(rows):
            xb_ref[rows, :] = x_ref[rows, :].astype(BF16)
            o_ref[rows, :] = jnp.zeros((slab, o_ref.shape[1]), F32)

        _row_slabs(tm, slab, init)

    w1b_ref[...] = w1_ref[...].astype(BF16)
    w2b_ref[...] = w2_ref[...].astype(BF16)

    def accumulate(rows):
        h = jnp.maximum(_dot(xb_ref[rows, :], w1b_ref[...]), 0.0)
        o_ref[rows, :] += _dot((h * h).astype(BF16), w2b_ref[...])

    _row_slabs(tm, mm_slab, accumulate, unroll=True)

    @pl.when(f == pl.num_programs(1) - 1)
    def _():
        def finish(rows):
            o_ref[rows, :] = _layer_norm(ALPHA * x_ref[rows, :] + o_ref[rows, :], g_ref[...], beta_ref[...])

        _row_slabs(tm, slab, finish)


def _ffn_ln(x, w1_all, w2_all, g_all, beta_all, layer, tm, tf):
    m, d = x.shape
    ff = w1_all.shape[2]
    slab = min(tm, 256)
    mm_slab = min(tm, 128)
    vmem = (2 * tm * d * 4 + tm * d * 2 + 4 * d * tf * 4 + 2 * d * tf * 2 + 2 * tm * d * 4
            + mm_slab * tf * 8 + 2 * mm_slab * d * 4 + 4 * slab * d * 4 + 4 * 2**20)
    return pl.pallas_call(
        functools.partial(_ffn_ln_kernel, slab=slab, mm_slab=mm_slab),
        grid=(m // tm, ff // tf),
        in_specs=[pl.BlockSpec((tm, d), lambda i, f: (i, 0)),
                  pl.BlockSpec((None, d, tf), lambda i, f: (layer, 0, f)),
                  pl.BlockSpec((None, tf, d), lambda i, f: (layer, f, 0)),
                  pl.BlockSpec((None, 1, d), lambda i, f: (layer, 0, 0)),
                  pl.BlockSpec((None, 1, d), lambda i, f: (layer, 0, 0))],
        out_specs=pl.BlockSpec((tm, d), lambda i, f: (i, 0)),
        out_shape=jax.ShapeDtypeStruct((m, d), F32),
        scratch_shapes=[pltpu.VMEM((tm, d), BF16), pltpu.VMEM((d, tf), BF16), pltpu.VMEM((tf, d), BF16)],
        compiler_params=_cparams(("parallel", "arbitrary"), vmem),
        name="ffn_ln",
    )(x, w1_all, w2_all, g_all, beta_all)


def _sortable_key(score):
    score = jnp.where(score == 0.0, 0.0, score)
    bits = lax.bitcast_convert_type(score, I32)
    return jnp.where(bits < 0, bits ^ jnp.int32(0x7FFFFFFF), bits)


def _fold_lanes(m):
    acc = m[:, 0:V7X_LANES]
    for c in range(1, m.shape[1] // V7X_LANES):
        acc = acc + m[:, c * V7X_LANES:(c + 1) * V7X_LANES]
    return acc


def _fold_sublanes(m):
    return jnp.sum(m.reshape(m.shape[0] // V7X_SUBLANES, V7X_SUBLANES, m.shape[1]), axis=0)


def _select_topk(load_group, n_groups, n_queries, topk, index_bits, key_axis):
    if key_axis == 1:
        state, acc_shape, fold = (n_queries, 1), (n_queries, V7X_LANES), _fold_lanes
        group_shape = (n_queries, KEY_GROUP)
    else:
        state, acc_shape, fold = (1, n_queries), (V7X_SUBLANES, n_queries), _fold_sublanes
        group_shape = (KEY_GROUP, n_queries)

    def count(pred):
        def body(g, acc):
            return acc + fold(pred(load_group(g), g).astype(I32))

        acc = lax.fori_loop(0, n_groups, body, jnp.zeros(acc_shape, I32))
        return jnp.sum(acc, axis=key_axis, keepdims=True)

    def thr_pass(p, thr):
        cand = thr + lax.shift_left(jnp.int32(1), jnp.int32(31) - p)
        c = count(lambda k, g: k >= cand)
        return jnp.where(c >= topk, cand, thr)

    thr = lax.fori_loop(0, 32, thr_pass, jnp.full(state, INT_MIN, I32))
    c_ge = count(lambda k, g: k >= thr)
    c_gt = count(lambda k, g: k > thr)
    need = topk - c_gt
    has_tie = jnp.max(jnp.where((c_ge > topk) & (thr > INT_MIN), 1, 0)) > 0
    all_idx = jnp.full(state, (1 << index_bits) - 1, I32)

    def tie_path():
        pos = lax.broadcasted_iota(I32, group_shape, key_axis)

        def idx_pass(p, last):
            cand = last + lax.shift_left(jnp.int32(1), jnp.int32(index_bits - 1) - p)
            c = count(lambda k, g: (k == thr) & (pos + g * KEY_GROUP < cand))
            return jnp.where(c <= need - 1, cand, last)

        return lax.fori_loop(0, index_bits, idx_pass, jnp.zeros(state, I32))

    last = lax.cond(has_tie, tie_path, lambda: all_idx)
    return thr, last


def _selected_bias(keys, idx, thr, last):
    sel = ((keys > thr) | ((keys == thr) & (idx <= last))) & (keys > INT_MIN)
    return jnp.where(sel, 0.0, NEG_BIG).astype(F32)


def _stack_heads(qb, group, rows_per_head):
    rep = N_HEADS // N_KV_HEADS
    return jnp.concatenate(
        [qb[:, (group * rep + r) * HEAD_DIM:(group * rep + r + 1) * HEAD_DIM] for r in range(rep)], axis=0)


def _softmax_step(s, v_bf16, m_ref, l_ref, acc_ref, g):
    m_old = m_ref[g]
    m_new = jnp.maximum(m_old, jnp.max(s, axis=1, keepdims=True))
    p = jnp.exp(s - m_new)
    alpha = jnp.exp(m_old - m_new)
    l_ref[g] = alpha * l_ref[g] + jnp.sum(p, axis=1, keepdims=True)
    acc_ref[g] = alpha * acc_ref[g] + _dot(p.astype(BF16), v_bf16)
    m_ref[g] = m_new


def _dsa_prompt_kernel(q_ref, qi_ref, sq_ref, kv_ref, sk_ref, o_ref,
                       kb_ref, vt_ref, kib_ref, keys_ref, m_ref, l_ref, acc_ref, *, topk, index_bits):
    i = pl.program_id(1)
    tq = Q_BLOCK
    rep = N_HEADS // N_KV_HEADS
    seq = kv_ref.shape[0]

    @pl.when(i == 0)
    def _():
        kb_ref[...] = kv_ref[:, 0:KV_W].astype(BF16)
        kib_ref[...] = sk_ref[:, SMALL_KI:SMALL_KI + IDX_DIM].astype(BF16)

        def transpose_v(c, carry):
            k0 = pl.multiple_of(c * KEY_GROUP, KEY_GROUP)
            vt_ref[:, pl.ds(k0, KEY_GROUP)] = kv_ref[pl.ds(k0, KEY_GROUP), KV_W:2 * KV_W].T.astype(BF16)
            return carry

        lax.fori_loop(0, seq // KEY_GROUP, transpose_v, 0)

    n_groups = lax.shift_right_logical(i * tq + tq + KEY_GROUP - 1, int(math.log2(KEY_GROUP)))
    q_pos = i * tq + lax.broadcasted_iota(I32, (KEY_GROUP, tq), 1)
    key_row = lax.broadcasted_iota(I32, (KEY_GROUP, tq), 0)

    qi_t = qi_ref[...].T.astype(BF16)
    qi_all = jnp.concatenate([qi_t[h * IDX_DIM:(h + 1) * IDX_DIM, :] for h in range(N_IDX_HEADS)], axis=1)
    wi_t = sq_ref[...].T[SMALL_WI:SMALL_WI + N_IDX_HEADS, :]

    def score_body(g, carry):
        k0 = pl.multiple_of(g * KEY_GROUP, KEY_GROUP)
        dots = jnp.maximum(_dot(kib_ref[pl.ds(k0, KEY_GROUP), :], qi_all), 0.0)
        sc = jnp.zeros((KEY_GROUP, tq), F32)
        for h in range(N_IDX_HEADS):
            sc = sc + wi_t[h:h + 1, :] * dots[:, h * tq:(h + 1) * tq]
        keys = _sortable_key(sc * IDX_SCALE)
        keys_ref[pl.ds(k0, KEY_GROUP), :] = jnp.where(key_row + k0 <= q_pos, keys, INT_MIN)
        return carry

    lax.fori_loop(0, n_groups, score_body, 0)

    def load_group(g):
        return keys_ref[pl.ds(pl.multiple_of(g * KEY_GROUP, KEY_GROUP), KEY_GROUP), :]

    thr, last = _select_topk(load_group, n_groups, tq, topk, index_bits, key_axis=0)

    scale = HEAD_DIM ** -0.5
    q_t = (q_ref[...] * scale).T.astype(BF16)
    qg_t = [jnp.concatenate([q_t[(g * rep + r) * HEAD_DIM:(g * rep + r + 1) * HEAD_DIM, :] for r in range(rep)], axis=1)
            for g in range(N_KV_HEADS)]
    m_ref[...] = jnp.full(m_ref.shape, NEG_BIG, F32)
    l_ref[...] = jnp.zeros(l_ref.shape, F32)
    acc_ref[...] = jnp.zeros(acc_ref.shape, F32)

    def attn_body(g, carry):
        k0 = pl.multiple_of(g * KEY_GROUP, KEY_GROUP)
        bias = _selected_bias(keys_ref[pl.ds(k0, KEY_GROUP), :], key_row + k0, thr, last)
        bias = jnp.concatenate([bias] * rep, axis=1)
        for kvh in range(N_KV_HEADS):
            kc = kb_ref[pl.ds(k0, KEY_GROUP), kvh * HEAD_DIM:(kvh + 1) * HEAD_DIM]
            vc_t = vt_ref[kvh * HEAD_DIM:(kvh + 1) * HEAD_DIM, pl.ds(k0, KEY_GROUP)]
            s = _dot(kc, qg_t[kvh]) + bias
            m_old = m_ref[kvh]
            m_new = jnp.maximum(m_old, jnp.max(s, axis=0, keepdims=True))
            p = jnp.exp(s - m_new)
            alpha = jnp.exp(m_old - m_new)
            l_ref[kvh] = alpha * l_ref[kvh] + jnp.sum(p, axis=0, keepdims=True)
            acc_ref[kvh] = alpha * acc_ref[kvh] + _dot(vc_t, p.astype(BF16))
            m_ref[kvh] = m_new
        return carry

    lax.fori_loop(0, n_groups, attn_body, 0)

    for kvh in range(N_KV_HEADS):
        out_t = acc_ref[kvh] / l_ref[kvh]
        for r in range(rep):
            h = kvh * rep + r
            o_ref[:, h * HEAD_DIM:(h + 1) * HEAD_DIM] = out_t[:, r * tq:(r + 1) * tq].T.astype(o_ref.dtype)


def _dsa_prompt(h_main, h_small, batch, seq):
    tq = Q_BLOCK
    nq = seq // tq
    topk = min(TOPK_MAX, seq // 4)
    cols = N_HEADS // N_KV_HEADS * tq
    kv_blk = 2 * KV_W
    kernel = functools.partial(_dsa_prompt_kernel, topk=topk, index_bits=int(math.log2(seq)))
    vmem = (2 * seq * kv_blk * 4 + seq * kv_blk * 2 + 2 * seq * AB_SMALL_W * 4 + seq * V7X_LANES * 2 + tq * seq * 4
            + 2 * tq * (D_ATTN + IDX_W + AB_SMALL_W) * 4 + 2 * tq * D_ATTN * 2
            + N_KV_HEADS * cols * (2 * V7X_SUBLANES + HEAD_DIM) * 4 + 8 * cols * KEY_GROUP * 4 + 4 * 2**20)
    return pl.pallas_call(
        kernel,
        grid=(batch, nq),
        in_specs=[pl.BlockSpec((tq, D_ATTN), lambda b, i: (b * nq + i, 0)),
                  pl.BlockSpec((tq, IDX_W), lambda b, i: (b * nq + i, (D_ATTN + 2 * KV_W) // IDX_W)),
                  pl.BlockSpec((tq, AB_SMALL_W), lambda b, i: (b * nq + i, 0)),
                  pl.BlockSpec((seq, kv_blk), lambda b, i: (b, D_ATTN // kv_blk)),
                  pl.BlockSpec((seq, AB_SMALL_W), lambda b, i: (b, 0))],
        out_specs=pl.BlockSpec((tq, D_ATTN), lambda b, i: (b * nq + i, 0)),
        out_shape=jax.ShapeDtypeStruct((batch * seq, D_ATTN), BF16),
        scratch_shapes=[pltpu.VMEM((seq, KV_W), BF16),
                        pltpu.VMEM((KV_W, seq), BF16),
                        pltpu.VMEM((seq, IDX_DIM), BF16),
                        pltpu.VMEM((seq, tq), I32),
                        pltpu.VMEM((N_KV_HEADS, 1, cols), F32),
                        pltpu.VMEM((N_KV_HEADS, 1, cols), F32),
                        pltpu.VMEM((N_KV_HEADS, HEAD_DIM, cols), F32)],
        compiler_params=_cparams(("parallel", "arbitrary"), vmem),
        name="dsa_prompt",
    )(h_main, h_main, h_small, h_main, h_small)


def _dsa_sample_scores_kernel(pt_ref, qi_ref, sq_ref, *refs, pages_per_step):
    page_refs = refs[:pages_per_step]
    o_ref = refs[pages_per_step]
    n = qi_ref.shape[0]
    qib = qi_ref[...].astype(BF16)
    qs = jnp.concatenate([qib[:, h * IDX_DIM:(h + 1) * IDX_DIM] for h in range(N_IDX_HEADS)], axis=0)
    wi = sq_ref[:, SMALL_WI:SMALL_WI + N_IDX_HEADS]
    for t in range(pages_per_step):
        dots = jnp.maximum(_dot(qs, page_refs[t][...].astype(BF16)), 0.0)
        sc = jnp.zeros((n, PAGE_SIZE), F32)
        for h in range(N_IDX_HEADS):
            sc = sc + wi[:, h:h + 1] * dots[h * n:(h + 1) * n, :]
        o_ref[:, t * PAGE_SIZE:(t + 1) * PAGE_SIZE] = sc * IDX_SCALE


def _dsa_sample_scores(page_table, h_main, h_small, cache_ki, layer, n, pages_per_step):
    batch, n_pages = page_table.shape
    steps = n_pages // pages_per_step
    kernel = functools.partial(_dsa_sample_scores_kernel, pages_per_step=pages_per_step)

    def page_spec(t):
        return pl.BlockSpec((None, None, IDX_DIM, PAGE_SIZE),
                            lambda b, s, pt: (layer, pt[b, s * pages_per_step + t], 0, 0))

    grid_spec = pltpu.PrefetchScalarGridSpec(
        num_scalar_prefetch=1,
        grid=(batch, steps),
        in_specs=[pl.BlockSpec((n, IDX_W), lambda b, s, pt: (b, (D_ATTN + 2 * KV_W) // IDX_W)),
                  pl.BlockSpec((n, AB_SMALL_W), lambda b, s, pt: (b, 0))]
                 + [page_spec(t) for t in range(pages_per_step)],
        out_specs=pl.BlockSpec((None, n, pages_per_step * PAGE_SIZE), lambda b, s, pt: (b, 0, s)),
    )
    vmem = 4 * pages_per_step * PAGE_SIZE * V7X_LANES * 4 + 4 * n * pages_per_step * PAGE_SIZE * 4 + 8 * 2**20
    return pl.pallas_call(
        kernel,
        grid_spec=grid_spec,
        out_shape=jax.ShapeDtypeStruct((batch, n, n_pages * PAGE_SIZE), F32),
        compiler_params=_cparams(("parallel", "arbitrary"), vmem),
        name="dsa_sample_scores",
    )(page_table, h_main, h_small, *([cache_ki] * pages_per_step))


def _dsa_sample_attend_kernel(pt_ref, sc_ref, q_ref, qi_ref, sq_ref, kvn_ref, *refs,
                              pages_per_step, topk, index_bits, past):
    k_refs = refs[:pages_per_step]
    v_refs = refs[pages_per_step:2 * pages_per_step]
    o_ref = refs[2 * pages_per_step]
    keys_ref, thr_ref, last_ref, m_ref, l_ref, acc_ref, pad_ref = refs[2 * pages_per_step + 1:]
    s_idx = pl.program_id(1)
    n = q_ref.shape[0]
    rep = N_HEADS // N_KV_HEADS
    n_groups = keys_ref.shape[1] // KEY_GROUP
    lane_p = lax.broadcasted_iota(I32, (n, PAGE_SIZE), 1)
    scale = HEAD_DIM ** -0.5

    @pl.when(s_idx == 0)
    def _():
        keys_ref[...] = jnp.full(keys_ref.shape, INT_MIN, I32)
        keys_ref[:, 0:past] = _sortable_key(sc_ref[...])
        qib = qi_ref[...].astype(BF16)
        wi = sq_ref[:, SMALL_WI:SMALL_WI + N_IDX_HEADS]
        pad_ref[...] = jnp.zeros(pad_ref.shape, F32)
        pad_ref[0:n, 0:IDX_DIM] = sq_ref[:, SMALL_KI:SMALL_KI + IDX_DIM]
        ki_new = pad_ref[:, 0:IDX_DIM].astype(BF16)
        sc = jnp.zeros((n, PAGE_SIZE), F32)
        for h in range(N_IDX_HEADS):
            dots = _dot_nt(qib[:, h * IDX_DIM:(h + 1) * IDX_DIM], ki_new)
            sc = sc + wi[:, h:h + 1] * jnp.maximum(dots, 0.0)
        q_row = lax.broadcasted_iota(I32, (n, PAGE_SIZE), 0)
        keys_ref[:, past:past + PAGE_SIZE] = jnp.where(lane_p <= q_row, _sortable_key(sc * IDX_SCALE), INT_MIN)

        def load_group(g):
            return keys_ref[:, pl.ds(pl.multiple_of(g * KEY_GROUP, KEY_GROUP), KEY_GROUP)]

        thr, last = _select_topk(load_group, n_groups, n, topk, index_bits, key_axis=1)
        thr_ref[...] = thr
        last_ref[...] = last
        m_ref[...] = jnp.full(m_ref.shape, NEG_BIG, F32)
        l_ref[...] = jnp.zeros(l_ref.shape, F32)
        acc_ref[...] = jnp.zeros(acc_ref.shape, F32)

    thr = thr_ref[...]
    last = last_ref[...]
    qb = q_ref[...].astype(BF16)
    qg = [_stack_heads(qb, g, n) for g in range(N_KV_HEADS)]

    def bias_at(k0):
        keys = keys_ref[:, pl.ds(k0, PAGE_SIZE)]
        b = _selected_bias(keys, lane_p + k0, thr, last)
        return jnp.concatenate([b] * rep, axis=0)

    base = s_idx * (pages_per_step * PAGE_SIZE)
    bias = jnp.concatenate([bias_at(pl.multiple_of(base + t * PAGE_SIZE, PAGE_SIZE)) for t in range(pages_per_step)], axis=1)
    for kvh in range(N_KV_HEADS):
        head_rows = pl.ds(kvh, PAGE_SIZE, stride=N_KV_HEADS)
        kc = jnp.concatenate([k_refs[t][head_rows, :].astype(BF16) for t in range(pages_per_step)], axis=0)
        vc = jnp.concatenate([v_refs[t][head_rows, :].astype(BF16) for t in range(pages_per_step)], axis=0)
        s = _dot_nt(qg[kvh], kc) * scale + bias
        _softmax_step(s, vc, m_ref, l_ref, acc_ref, kvh)

    @pl.when(s_idx == pl.num_programs(1) - 1)
    def _():
        bias_n = bias_at(past)
        for kvh in range(N_KV_HEADS):
            pad_ref[0:n, :] = kvn_ref[:, kvh * HEAD_DIM:(kvh + 1) * HEAD_DIM]
            kc = pad_ref[...].astype(BF16)
            pad_ref[0:n, :] = kvn_ref[:, KV_W + kvh * HEAD_DIM:KV_W + (kvh + 1) * HEAD_DIM]
            vc = pad_ref[...].astype(BF16)
            s = _dot_nt(qg[kvh], kc) * scale + bias_n
            _softmax_step(s, vc, m_ref, l_ref, acc_ref, kvh)
        for kvh in range(N_KV_HEADS):
            out = acc_ref[kvh] / l_ref[kvh]
            for r in range(rep):
                h = kvh * rep + r
                o_ref[:, h * HEAD_DIM:(h + 1) * HEAD_DIM] = out[r * n:(r + 1) * n, :].astype(o_ref.dtype)


def _dsa_sample_attend(page_table, scores, h_main, h_small, cache_k, cache_v, layer, n, pages_per_step):
    batch, n_pages = page_table.shape
    past = n_pages * PAGE_SIZE
    steps = n_pages // pages_per_step
    topk = min(TOPK_MAX, (past + n) // 4)
    total = past + PAGE_SIZE
    padded = -(-total // KEY_GROUP) * KEY_GROUP
    index_bits = int(math.ceil(math.log2(padded)))
    rows = N_HEADS // N_KV_HEADS * n
    kernel = functools.partial(_dsa_sample_attend_kernel, pages_per_step=pages_per_step, topk=topk,
                               index_bits=index_bits, past=past)

    def page_spec(t):
        return pl.BlockSpec((None, None, PAGE_SIZE * N_KV_HEADS, HEAD_DIM),
                            lambda b, s, pt: (layer, pt[b, s * pages_per_step + t], 0, 0))

    kv_blk = 2 * KV_W
    grid_spec = pltpu.PrefetchScalarGridSpec(
        num_scalar_prefetch=1,
        grid=(batch, steps),
        in_specs=[pl.BlockSpec((None, n, past), lambda b, s, pt: (b, 0, 0)),
                  pl.BlockSpec((n, D_ATTN), lambda b, s, pt: (b, 0)),
                  pl.BlockSpec((n, IDX_W), lambda b, s, pt: (b, (D_ATTN + 2 * KV_W) // IDX_W)),
                  pl.BlockSpec((n, AB_SMALL_W), lambda b, s, pt: (b, 0)),
                  pl.BlockSpec((n, kv_blk), lambda b, s, pt: (b, D_ATTN // kv_blk))]
                 + [page_spec(t) for t in range(pages_per_step)] * 2,
        out_specs=pl.BlockSpec((n, D_ATTN), lambda b, s, pt: (b, 0)),
        scratch_shapes=[pltpu.VMEM((n, padded), I32),
                        pltpu.VMEM((n, 1), I32),
                        pltpu.VMEM((n, 1), I32),
                        pltpu.VMEM((N_KV_HEADS, rows, 1), F32),
                        pltpu.VMEM((N_KV_HEADS, rows, 1), F32),
                        pltpu.VMEM((N_KV_HEADS, rows, HEAD_DIM), F32),
                        pltpu.VMEM((PAGE_SIZE, HEAD_DIM), F32)],
    )
    vmem = (4 * 2 * pages_per_step * PAGE_SIZE * KV_W * 4 + 4 * n * padded * 4
            + 8 * rows * pages_per_step * PAGE_SIZE * 4 + 8 * 2**20)
    return pl.pallas_call(
        kernel,
        grid_spec=grid_spec,
        out_shape=jax.ShapeDtypeStruct((batch * n, D_ATTN), BF16),
        compiler_params=_cparams(("parallel", "arbitrary"), vmem),
        name="dsa_sample_attend",
    )(page_table, scores, h_main, h_main, h_small, h_main, *([cache_k] * pages_per_step), *([cache_v] * pages_per_step))


def _ssd_kernel(z_ref, xbc_ref, dtc_ref, dtr_ref, pre_ref, h0_ref, cw_ref, cb_ref, dtb_ref, alog_ref, dsk_ref,
                dtb_c_ref, alog_c_ref, g_ref, y_ref, hl_ref, state_ref, stage_ref, dtpad_ref, zpad_ref, *, t_in):
    c = pl.program_id(1)
    tt = SSD_CHUNK
    hpg = SSD_HEADS // SSD_GROUPS
    gw = hpg * SSD_HEAD_DIM

    @pl.when(c == 0)
    def _():
        state_ref[...] = h0_ref[...]
        stage_ref[0:V7X_SUBLANES, :] = pre_ref[...]
        if t_in < tt:
            stage_ref[V7X_SUBLANES:, :] = jnp.zeros((tt, CONV_DIM), F32)
            dtpad_ref[...] = jnp.zeros(dtpad_ref.shape, F32)
            zpad_ref[...] = jnp.zeros(zpad_ref.shape, F32)

    stage_ref[V7X_SUBLANES:V7X_SUBLANES + t_in, :] = xbc_ref[...]
    conv = cb_ref[...] + jnp.zeros((tt, CONV_DIM), F32)
    for j in range(CONV_W):
        off = V7X_SUBLANES - (CONV_W - 1) + j
        conv = conv + cw_ref[j:j + 1, :] * stage_ref[off:off + tt, :]
    if t_in == tt:
        stage_ref[0:V7X_SUBLANES, :] = xbc_ref[tt - V7X_SUBLANES:tt, :]
    xbc = conv * jax.nn.sigmoid(conv)
    xs = xbc[:, 0:D_SSD]

    row_valid = lax.broadcasted_iota(I32, (tt, 1), 0) < t_in
    lane_valid = lax.broadcasted_iota(I32, (1, tt), 1) < t_in
    if t_in < tt:
        dtpad_ref[0:t_in, :] = dtc_ref[...]
        dt_raw_c = dtpad_ref[...]
        dtpad_ref[0:SSD_HEADS, 0:t_in] = dtr_ref[...]
        dt_raw_r = dtpad_ref[0:SSD_HEADS, :]
        zpad_ref[0:t_in, :] = z_ref[...]
        z = zpad_ref[...]
    else:
        dt_raw_c = dtc_ref[...]
        dt_raw_r = dtr_ref[...]
        z = z_ref[...]

    hl = lax.broadcasted_iota(I32, (1, AB_SMALL_W), 1)
    head_lane = (hl >= SMALL_DT) & (hl < SMALL_DT + SSD_HEADS)
    dt_c = jnp.where(row_valid & head_lane, jax.nn.softplus(dt_raw_c + dtb_ref[...]), 0.0)
    dt_r = jnp.where(lane_valid, jax.nn.softplus(dt_raw_r + dtb_c_ref[...]), 0.0)
    a_c = -jnp.exp(alog_ref[...])
    a_r = -jnp.exp(alog_c_ref[...])
    ri = lax.broadcasted_iota(I32, (tt, tt), 0)
    ci = lax.broadcasted_iota(I32, (tt, tt), 1)
    tril = ri >= ci
    acum_c = _dot_exact(jnp.where(tril, 1.0, 0.0).astype(F32), dt_c * a_c)
    acum_r = _dot_exact(dt_r * a_r, jnp.where(ri <= ci, 1.0, 0.0).astype(F32))
    acum_last = acum_c[tt - 1:tt, :]

    eh = lax.broadcasted_iota(I32, (AB_SMALL_W, D_SSD), 0) - SMALL_DT
    ef = lax.broadcasted_iota(I32, (AB_SMALL_W, D_SSD), 1)
    expand = jnp.where(lax.shift_right_logical(ef, int(math.log2(SSD_HEAD_DIM))) == eh, 1.0, 0.0).astype(F32)
    decay_in = _dot_exact(jnp.exp(acum_c), expand)
    decay_out = _dot_exact(jnp.exp(acum_last - acum_c) * dt_c, expand)
    chunk_decay = _dot_exact(jnp.exp(acum_last) + jnp.zeros((V7X_SUBLANES, 1), F32), expand)[0:1, :]
    d_skip = _dot_exact(dsk_ref[...] + jnp.zeros((V7X_SUBLANES, 1), F32), expand)[0:1, :]

    xs_b = xs.astype(BF16)
    xw_b = (xs * decay_out).astype(BF16)
    lane128 = lax.broadcasted_iota(I32, (tt, 2 * SSD_HEAD_DIM), 1)
    y_parts = []
    new_state = []
    for g in range(SSD_GROUPS):
        bm = xbc[:, D_SSD + g * SSD_STATE:D_SSD + (g + 1) * SSD_STATE].astype(BF16)
        cm = xbc[:, D_SSD + (SSD_GROUPS + g) * SSD_STATE:D_SSD + (SSD_GROUPS + g + 1) * SSD_STATE].astype(BF16)
        cb = _dot_nt(cm, bm)
        st_g = state_ref[:, g * gw:(g + 1) * gw]
        y_off = _dot(cm, st_g.astype(BF16)) * decay_in[:, g * gw:(g + 1) * gw]
        new_state.append(st_g * chunk_decay[:, g * gw:(g + 1) * gw] + _dot_tn(bm, xw_b[:, g * gw:(g + 1) * gw]))
        for pair in range(hpg // 2):
            ms = []
            for hh in range(2):
                h = g * hpg + 2 * pair + hh
                seg = jnp.where(tril, acum_c[:, SMALL_DT + h:SMALL_DT + h + 1] - acum_r[h:h + 1, :], -jnp.inf)
                ms.append((cb * jnp.exp(seg) * dt_r[h:h + 1, :]).astype(BF16))
            col = (g * hpg + 2 * pair) * SSD_HEAD_DIM
            xp = xs_b[:, col:col + 2 * SSD_HEAD_DIM]
            zero = jnp.zeros_like(xp)
            rhs = jnp.concatenate([jnp.where(lane128 < SSD_HEAD_DIM, xp, zero),
                                   jnp.where(lane128 >= SSD_HEAD_DIM, xp, zero)], axis=0)
            y_parts.append(_dot(jnp.concatenate(ms, axis=1), rhs) + y_off[:, col - g * gw:col - g * gw + 2 * SSD_HEAD_DIM])
    for g in range(SSD_GROUPS):
        state_ref[:, g * gw:(g + 1) * gw] = new_state[g]

    y = jnp.concatenate(y_parts, axis=1) + d_skip * xs
    y = y * (z * jax.nn.sigmoid(z))
    y = y * lax.rsqrt(jnp.mean(y * y, axis=-1, keepdims=True) + EPS) * g_ref[...]
    y_ref[...] = y[0:t_in, :].astype(y_ref.dtype)

    @pl.when(c == pl.num_programs(1) - 1)
    def _():
        hl_ref[...] = state_ref[...]


def _ssd(h_main, h_small, dt_rows, prefix, h0, conv_w, conv_b, dt_bias, a_log, d_skip, ssd_g, batch, n_chunks, t_in):
    tt = SSD_CHUNK
    kernel = functools.partial(_ssd_kernel, t_in=t_in)
    z_blk = (D_ATTN + 2 * KV_W + IDX_W) // D_SSD
    xbc_blk = (D_ATTN + 2 * KV_W + IDX_W + D_SSD) // CONV_DIM
    row = lambda v: v.reshape(1, -1)
    col = lambda v: v.reshape(-1, 1)
    head_row = lambda v: jnp.pad(v.reshape(1, -1), ((0, 0), (SMALL_DT, AB_SMALL_W - SMALL_DT - SSD_HEADS)))
    full = lambda shape: pl.BlockSpec(shape, lambda b, c: (0,) * len(shape))
    vmem = 40 * tt * CONV_DIM * 4 + 8 * SSD_STATE * D_SSD * 4 + 8 * 2**20
    return pl.pallas_call(
        kernel,
        grid=(batch, n_chunks),
        in_specs=[pl.BlockSpec((t_in, D_SSD), lambda b, c: (b * n_chunks + c, z_blk)),
                  pl.BlockSpec((t_in, CONV_DIM), lambda b, c: (b * n_chunks + c, xbc_blk)),
                  pl.BlockSpec((t_in, AB_SMALL_W), lambda b, c: (b * n_chunks + c, 0)),
                  pl.BlockSpec((None, SSD_HEADS, t_in), lambda b, c: (b * n_chunks + c, 0, 0)),
                  pl.BlockSpec((None, V7X_SUBLANES, CONV_DIM), lambda b, c: (b, 0, 0)),
                  pl.BlockSpec((None, SSD_STATE, D_SSD), lambda b, c: (b, 0, 0)),
                  full((CONV_W, CONV_DIM)), full((1, CONV_DIM)), full((1, AB_SMALL_W)), full((1, AB_SMALL_W)),
                  full((1, AB_SMALL_W)), full((SSD_HEADS, 1)), full((SSD_HEADS, 1)), full((1, D_SSD))],
        out_specs=[pl.BlockSpec((t_in, D_SSD), lambda b, c: (b * n_chunks + c, 0)),
                   pl.BlockSpec((None, SSD_STATE, D_SSD), lambda b, c: (b, 0, 0))],
        out_shape=[jax.ShapeDtypeStruct((batch * n_chunks * t_in, D_SSD), BF16),
                   jax.ShapeDtypeStruct((batch, SSD_STATE, D_SSD), F32)],
        scratch_shapes=[pltpu.VMEM((SSD_STATE, D_SSD), F32),
                        pltpu.VMEM((V7X_SUBLANES + tt, CONV_DIM), F32),
                        pltpu.VMEM((tt, AB_SMALL_W), F32),
                        pltpu.VMEM((tt, D_SSD), F32)],
        compiler_params=_cparams(("parallel", "arbitrary"), vmem),
        name="ssd",
    )(h_main, h_main, h_small, dt_rows, prefix, h0, conv_w, row(conv_b), head_row(dt_bias), head_row(a_log), head_row(d_skip),
      col(dt_bias), col(a_log), row(ssd_g))


def _cd_kernel(p_ref, u_ref, v_ref, pre_ref, pw_ref, ps_ref, lg_ref, lb_ref, sw_ref, sb_ref,
               pooled_ref, gated_ref, vout_ref, stage_ref, upad_ref, vpad_ref, *, t_in, prefix_len):
    c = pl.program_id(1)
    tt = SGU_CHUNK
    hist = max(POOL_WINDOWS)

    @pl.when(c == 0)
    def _():
        stage_ref[0:hist, :] = pre_ref[...]
        if t_in < tt:
            stage_ref[hist:, :] = jnp.zeros((tt, POOL_W), F32)
            upad_ref[...] = jnp.zeros(upad_ref.shape, F32)
            vpad_ref[...] = jnp.zeros(vpad_ref.shape, F32)

    stage_ref[hist:hist + t_in, :] = p_ref[...]
    end = prefix_len + c * tt + lax.broadcasted_iota(I32, (tt, 1), 0) + 1
    pooled = []
    for j, w in enumerate(POOL_WINDOWS):
        cols = slice(j * POOL_GW, (j + 1) * POOL_GW)
        cur = stage_ref[hist:hist + tt, cols]
        tot = cur
        for k in range(1, w):
            tot = tot + stage_ref[hist - k:hist - k + tt, cols]
        cnt = jnp.minimum(end, w).astype(F32)
        d = (tot / cnt - cur).astype(BF16)
        pooled.append(_dot(d, pw_ref[j].astype(BF16)))
    if t_in == tt:
        stage_ref[0:hist, :] = p_ref[tt - hist:tt, :]
    pooled = jnp.concatenate(pooled, axis=1) * ps_ref[...]
    pooled_ref[...] = pooled[0:t_in, :].astype(pooled_ref.dtype)

    gelu = lambda x: x * (lax.erf(x / math.sqrt(2.0)) + 1.0) / 2.0
    u = gelu(u_ref[...])
    v = _layer_norm(gelu(v_ref[...]), lg_ref[...], lb_ref[...])
    vout_ref[...] = v
    if t_in < tt:
        upad_ref[0:t_in, :] = u
        vpad_ref[0:t_in, :] = v
        u = upad_ref[...]
        v = vpad_ref[...]
    ri = lax.broadcasted_iota(I32, (tt, tt), 0)
    ci = lax.broadcasted_iota(I32, (tt, tt), 1)
    vb = v.astype(BF16)
    gates = []
    for g in range(SGU_GROUPS):
        wm = jnp.where(ri >= ci, sw_ref[g], 0.0).astype(BF16)
        gates.append(_dot(wm, vb[:, g * SGU_GW:(g + 1) * SGU_GW]) + sb_ref[:, g:g + 1])
    gated = u * jnp.concatenate(gates, axis=1)
    gated_ref[...] = gated[0:t_in, :].astype(gated_ref.dtype)


def _cd_mixer(h, prefix, pool_w, pool_scale, ln_g, ln_b, sgu_w, sgu_b, batch, n_chunks, t_in, prefix_len):
    tt = SGU_CHUNK
    hist = max(POOL_WINDOWS)
    kernel = functools.partial(_cd_kernel, t_in=t_in, prefix_len=prefix_len)
    row = lambda v: v.reshape(1, -1)
    full = lambda shape: pl.BlockSpec(shape, lambda b, c: (0,) * len(shape))
    m = batch * n_chunks * t_in
    vmem = 40 * tt * POOL_W * 4 + 4 * POOL_GW * POOL_GW * len(POOL_WINDOWS) * 4 + 8 * 2**20
    return pl.pallas_call(
        kernel,
        grid=(batch, n_chunks),
        in_specs=[pl.BlockSpec((t_in, POOL_W), lambda b, c: (b * n_chunks + c, 0)),
                  pl.BlockSpec((t_in, SGU_W), lambda b, c: (b * n_chunks + c, 1)),
                  pl.BlockSpec((t_in, SGU_W), lambda b, c: (b * n_chunks + c, 2)),
                  pl.BlockSpec((None, hist, POOL_W), lambda b, c: (b, 0, 0)),
                  full((len(POOL_WINDOWS), POOL_GW, POOL_GW)), full((1, POOL_W)), full((1, SGU_W)), full((1, SGU_W)),
                  full((SGU_GROUPS, tt, tt)), full((tt, SGU_GROUPS))],
        out_specs=[pl.BlockSpec((t_in, POOL_W), lambda b, c: (b * n_chunks + c, 0)),
                   pl.BlockSpec((t_in, SGU_W), lambda b, c: (b * n_chunks + c, 0)),
                   pl.BlockSpec((t_in, SGU_W), lambda b, c: (b * n_chunks + c, 0))],
        out_shape=[jax.ShapeDtypeStruct((m, POOL_W), BF16),
                   jax.ShapeDtypeStruct((m, SGU_W), BF16),
                   jax.ShapeDtypeStruct((m, SGU_W), F32)],
        scratch_shapes=[pltpu.VMEM((hist + tt, POOL_W), F32),
                        pltpu.VMEM((tt, SGU_W), F32),
                        pltpu.VMEM((tt, SGU_W), F32)],
        compiler_params=_cparams(("parallel", "arbitrary"), vmem),
        name="cd_mixer",
    )(h, h, h, prefix, pool_w, row(pool_scale), row(ln_g), row(ln_b), sgu_w, sgu_b.T)


def _row_tile(m, cap):
    t = min(m, cap)
    while m % t:
        t //= 2
    return t


def _ab_layer(yp, ys, j, i, bp, sp, bs, ns, cache_k, cache_v, cache_idx_k, state_ssm, state_conv, page_table,
              w_in_ab, conv_w, conv_b, dt_bias, a_log, d_skip, ssd_norm_g, w_out_ab, ln_g, ln_b):
    o = [0]
    for s in AB_SIZES:
        o.append(o[-1] + s)
    w_rest = jnp.concatenate([w_in_ab[j, :, o[6]:o[8]], w_in_ab[j, :, o[4]:o[6]], w_in_ab[j, :, o[8]:o[9]],
                              jnp.zeros((D_MODEL, AB_SMALL_W - (o[6] - o[4]) - (o[9] - o[8])), w_in_ab.dtype)], axis=1)
    outs = []
    states = []
    for x, batch, n in ((yp, bp, sp), (ys, bs, ns)):
        m = batch * n
        hm, hs = _inproj_ab(x, w_in_ab, w_rest, j, _row_tile(m, 1024), 512)
        prompt = n % SSD_CHUNK == 0
        t_in = SSD_CHUNK if prompt else n
        n_chunks = n // t_in
        dt_rows = hs[:, SMALL_DT:SMALL_DT + SSD_HEADS].reshape(batch * n_chunks, t_in, SSD_HEADS).transpose(0, 2, 1)
        if prompt:
            attn = _dsa_prompt(hm, hs, batch, n)
            prefix = jnp.zeros((batch, V7X_SUBLANES, CONV_DIM), F32)
            h0 = jnp.zeros((batch, SSD_STATE, D_SSD), F32)
        else:
            pages = 16
            page_view = cache_k.shape[:2] + (PAGE_SIZE * N_KV_HEADS, HEAD_DIM)
            scores = _dsa_sample_scores(page_table, hm, hs, jnp.swapaxes(cache_idx_k, 2, 3), j, n, pages)
            attn = _dsa_sample_attend(page_table, scores, hm, hs, cache_k.reshape(page_view), cache_v.reshape(page_view),
                                      j, n, pages)
            prefix = jnp.concatenate([jnp.zeros((batch, V7X_SUBLANES - (CONV_W - 1), CONV_DIM), F32), state_conv[j]], axis=1)
            h0 = state_ssm[j].transpose(0, 3, 1, 2).reshape(batch, SSD_STATE, D_SSD)
        y_ssd, h_last = _ssd(hm, hs, dt_rows, prefix, h0, conv_w[j], conv_b[j], dt_bias[j], a_log[j], d_skip[j],
                             ssd_norm_g[j], batch, n_chunks, t_in)
        outs.append(_outproj_ln(attn, y_ssd, w_out_ab, j, x, ln_g, ln_b, i, _row_tile(m, 512)))
        k = hm[:, o[1]:o[2]].reshape(batch, n, N_KV_HEADS, HEAD_DIM)
        v = hm[:, o[2]:o[3]].reshape(batch, n, N_KV_HEADS, HEAD_DIM)
        ki = hs[:, SMALL_KI:SMALL_KI + IDX_DIM].reshape(batch, n, IDX_DIM)
        ssm = h_last.reshape(batch, SSD_STATE, SSD_HEADS, SSD_HEAD_DIM).transpose(0, 2, 3, 1)
        xbc_off = o[4] + D_SSD
        conv_new = hm.reshape(batch, n, AB_MAIN_W)[:, n - (CONV_W - 1):, xbc_off:xbc_off + CONV_DIM]
        states.append((k, v, ki, ssm, conv_new))
    return outs, states


def _cd_layer(yp, ys, j, i, bp, sp, bs, ns, state_pool, w_in_cd, pool_w, pool_scale, sgu_ln_g, sgu_ln_b, sgu_w, sgu_b,
              w_out_cd, ln_g, ln_b):
    hist = max(POOL_WINDOWS)
    outs = []
    states = []
    for x, batch, n in ((yp, bp, sp), (ys, bs, ns)):
        m = batch * n
        h = _inproj(x, w_in_cd, j, _row_tile(m, 1024), 512)
        prompt = n % SGU_CHUNK == 0
        t_in = SGU_CHUNK if prompt else n
        n_chunks = n // t_in
        p = h.reshape(batch, n, h.shape[1])[:, max(n - POOL_STATE, 0):, 0:POOL_W]
        if prompt:
            prefix = jnp.zeros((batch, hist, POOL_W), F32)
            prefix_len = 0
            pool_state = p
        else:
            prefix = jnp.concatenate([jnp.zeros((batch, hist - POOL_STATE, POOL_W), F32), state_pool[j]], axis=1)
            prefix_len = POOL_STATE
            pool_state = jnp.concatenate([state_pool[j], p], axis=1)[:, n:]
        pooled, gated, v = _cd_mixer(h, prefix, pool_w[j], pool_scale[j], sgu_ln_g[j], sgu_ln_b[j], sgu_w[j], sgu_b[j],
                                     batch, n_chunks, t_in, prefix_len)
        outs.append(_outproj_ln(pooled, gated, w_out_cd, j, x, ln_g, ln_b, i, _row_tile(m, 512)))
        states.append((pool_state, v.reshape(batch, n, SGU_W)))
    return outs, states


def kernel(x_prompt, x_sample, cache_k, cache_v, cache_idx_k, state_ssm, state_conv, state_pool, page_table, ln1_g, ln1_b, ln2_g, ln2_b, w_in_ab, conv_w, conv_b, dt_bias, a_log, d_skip, ssd_norm_g, w_out_ab, w_in_cd, pool_w, pool_scale, sgu_ln_g, sgu_ln_b, sgu_w, sgu_b, w_out_cd, w_ff1, w_ff2):
    bp, sp, d = x_prompt.shape
    bs, ns, _ = x_sample.shape
    depth = w_ff1.shape[0]
    yp = x_prompt.reshape(bp * sp, d)
    ys = x_sample.reshape(bs * ns, d)
    ab_p, ab_s, cd_p, cd_s = [], [], [], []
    ln1 = (ln1_g.reshape(depth, 1, d), ln1_b.reshape(depth, 1, d))
    ln2 = (ln2_g.reshape(depth, 1, d), ln2_b.reshape(depth, 1, d))
    for i in range(depth):
        j = i // 2
        if i % 2 == 0:
            (yp, ys), (st_p, st_s) = _ab_layer(yp, ys, j, i, bp, sp, bs, ns, cache_k, cache_v, cache_idx_k, state_ssm,
                                               state_conv, page_table, w_in_ab, conv_w, conv_b, dt_bias, a_log, d_skip,
                                               ssd_norm_g, w_out_ab, *ln1)
            ab_p.append(st_p)
            ab_s.append(st_s)
        else:
            (yp, ys), (st_p, st_s) = _cd_layer(yp, ys, j, i, bp, sp, bs, ns, state_pool, w_in_cd, pool_w, pool_scale,
                                               sgu_ln_g, sgu_ln_b, sgu_w, sgu_b, w_out_cd, *ln1)
            cd_p.append(st_p)
            cd_s.append(st_s)
        yp = _ffn_ln(yp, w_ff1, w_ff2, *ln2, i, _row_tile(bp * sp, 1024), 256)
        ys = _ffn_ln(ys, w_ff1, w_ff2, *ln2, i, _row_tile(bs * ns, 1024), 256)
    stack = lambda lst, idx: jnp.stack([t[idx] for t in lst])
    return (yp.reshape(bp, sp, d), ys.reshape(bs, ns, d),
            stack(ab_p, 0), stack(ab_p, 1), stack(ab_p, 2), stack(ab_p, 3), stack(ab_p, 4), stack(cd_p, 0),
            stack(ab_s, 0), stack(ab_s, 1), stack(ab_s, 2), stack(ab_s, 3), stack(ab_s, 4), stack(cd_s, 0), stack(cd_s, 1))
```
